```python
import math
import jax
import jax.numpy as jnp
from jax import lax
import numpy as np

D_MODEL = 1024
BATCH = 8
SEQ = 4096
DEPTH = 2

HEAD_DIM = 64
N_HEADS_A = 8
N_KV_A = 2
HPG_A = N_HEADS_A // N_KV_A
N_HEADS_B = 8
N_HEADS = N_HEADS_A + N_HEADS_B
WIDTH_A = N_HEADS_A * HEAD_DIM
WIDTH_B = N_HEADS_B * HEAD_DIM
MIX_WIDTH = WIDTH_A + WIDTH_B
KV_A = N_KV_A * HEAD_DIM
IN_SPLITS = (WIDTH_A, KV_A, KV_A, KV_A, KV_A, KV_A, KV_A, N_HEADS_A * 3, WIDTH_B, WIDTH_B, WIDTH_B)
IN_WIDTH = sum(IN_SPLITS)

CMP_LEN = 32
CMP_STRIDE = 16
CMP_HIDDEN = 256
SEL_BLOCK = 64
SEL_TOPK = 16
WIN_A = 512
NSA_QBLOCK = 64

DIL_PAIRS = ((128, 1), (512, 4), (2048, 16))

N_BUCKETS = 32
MAX_DISTANCE = 2048

D_FF = 2816
N_EXPERTS = 8
TOP_K = 2
D_FF_EXPERT = 3584
MOE_CHUNK = 256

RMS_EPS = 1e-6
NEG = -1e30
SCALE = HEAD_DIM ** -0.5

kernel_name = 'hymba_nsa_dilated_moe_trunk'


def rmsnorm(x, g):
    xf = x.astype(jnp.float32)
    y = xf * lax.rsqrt(jnp.mean(xf * xf, axis=-1, keepdims=True) + RMS_EPS)
    return (y * g.astype(jnp.float32)).astype(x.dtype)


def t5_bucket(dist):
    dist = jnp.maximum(dist, 0)
    max_exact = N_BUCKETS // 2
    scaled = jnp.log(jnp.maximum(dist, 1).astype(jnp.float32) / max_exact) / math.log(MAX_DISTANCE / max_exact)
    large = jnp.minimum(max_exact + (scaled * (N_BUCKETS - max_exact)).astype(jnp.int32), N_BUCKETS - 1)
    return jnp.where(dist < max_exact, dist, large)


def masked_softmax(s, mask):
    p = jax.nn.softmax(jnp.where(mask, s, NEG), axis=-1)
    return jnp.where(mask, p, 0.0)


def swiglu(h, w_gate, w_up, w_down):
    return (jax.nn.silu(h @ w_gate) * (h @ w_up)) @ w_down


def compress_tokens(kv, pos_emb, w1, b1, w2):
    B, S, G, dh = kv.shape
    n_cmp = (S - CMP_LEN) // CMP_STRIDE + 1
    idx = jnp.arange(n_cmp)[:, None] * CMP_STRIDE + jnp.arange(CMP_LEN)[None, :]
    blocks = kv[:, idx] + pos_emb[:, None, :]
    flat = blocks.transpose(0, 3, 1, 2, 4).reshape(B, G, n_cmp, CMP_LEN * dh)
    return jax.nn.gelu(flat @ w1 + b1) @ w2


def nsa_mixer(q, k_c, v_c, k_s, v_s, k_w, v_w, gate_logits, q_norm, k_norm,
              cmp_pos, cmp_w1, cmp_b1, cmp_w2, tbl):
    B, S = q.shape[0], q.shape[1]
    G, H, dh, QA = N_KV_A, HPG_A, HEAD_DIM, NSA_QBLOCK
    qh = rmsnorm(q, q_norm).reshape(B, S, G, H, dh).transpose(0, 2, 3, 1, 4)
    kc = rmsnorm(compress_tokens(k_c, cmp_pos[0], cmp_w1[0], cmp_b1[0], cmp_w2[0]), k_norm[0])
    vc = compress_tokens(v_c, cmp_pos[1], cmp_w1[1], cmp_b1[1], cmp_w2[1])
    n_cmp = kc.shape[2]
    n_sb = S // SEL_BLOCK
    ks = rmsnorm(k_s, k_norm[1]).reshape(B, n_sb, SEL_BLOCK, G, dh).transpose(0, 3, 1, 2, 4)
    vs = v_s.reshape(B, n_sb, SEL_BLOCK, G, dh).transpose(0, 3, 1, 2, 4)
    pad = ((0, 0), (0, 0), (WIN_A, 0), (0, 0))
    kw = jnp.pad(rmsnorm(k_w, k_norm[2]).transpose(0, 2, 1, 3), pad)
    vw = jnp.pad(v_w.transpose(0, 2, 1, 3), pad)
    gates = jax.nn.sigmoid(gate_logits.astype(jnp.float32)).reshape(B, S, G, H, 3).transpose(0, 2, 3, 1, 4)
    tbl_g = tbl.reshape(G, H, N_BUCKETS)

    c_start = jnp.arange(n_cmp)[:, None] * CMP_STRIDE
    s_start = jnp.arange(n_sb)[None, :] * SEL_BLOCK
    overlap = jnp.clip(jnp.minimum(c_start + CMP_LEN, s_start + SEL_BLOCK) - jnp.maximum(c_start, s_start), 0, None)
    overlap = overlap.astype(jnp.float32) / CMP_LEN
    c_end = jnp.arange(n_cmp) * CMP_STRIDE + CMP_LEN - 1
    blk_j = jnp.arange(n_sb)
    k_sel = min(SEL_TOPK, n_sb)
    b_ix = jnp.arange(B)[:, None, None, None]
    g_ix = jnp.arange(G)[None, :, None, None]
    g5 = jnp.arange(G)[None, :, None, None, None]
    h5 = jnp.arange(H)[None, None, :, None, None]

    def block(i):
        t0 = i * QA
        t = t0 + jnp.arange(QA)
        qi = lax.dynamic_slice_in_dim(qh, t0, QA, axis=3)
        dist_c = t[:, None] - c_end[None, :]
        bias_c = tbl[:, t5_bucket(dist_c)].reshape(G, H, QA, n_cmp)
        s_c = jnp.einsum('bghqd,bgnd->bghqn', qi, kc).astype(jnp.float32) * SCALE + bias_c
        p_c = masked_softmax(s_c, dist_c >= 0)
        o_c = jnp.einsum('bghqn,bgnd->bghqd', p_c, vc)
        imp = jnp.einsum('bghqn,nj->bgqj', p_c, overlap)
        cur = t // SEL_BLOCK
        forced = (blk_j[None, :] == 0) | (blk_j[None, :] == cur[:, None]) | (blk_j[None, :] == cur[:, None] - 1)
        future = blk_j[None, :] * SEL_BLOCK > t[:, None]
        imp = jnp.where(forced, 1e6, jnp.where(future, -1e6, imp))
        _, sel = lax.top_k(imp, k_sel)
        k_g = ks[b_ix, g_ix, sel].reshape(B, G, QA, k_sel * SEL_BLOCK, dh)
        v_g = vs[b_ix, g_ix, sel].reshape(B, G, QA, k_sel * SEL_BLOCK, dh)
        pos = (sel[..., None] * SEL_BLOCK + jnp.arange(SEL_BLOCK)).reshape(B, G, QA, k_sel * SEL_BLOCK)
        dist_s = t[:, None] - pos
        bias_s = tbl_g[g5, h5, t5_bucket(dist_s)[:, :, None]]
        s_s = jnp.einsum('bghqd,bgqnd->bghqn', qi, k_g).astype(jnp.float32) * SCALE + bias_s
        p_s = masked_softmax(s_s, (dist_s >= 0)[:, :, None])
        o_s = jnp.einsum('bghqn,bgqnd->bghqd', p_s, v_g)
        k_wi = lax.dynamic_slice_in_dim(kw, t0, WIN_A + QA, axis=2)
        v_wi = lax.dynamic_slice_in_dim(vw, t0, WIN_A + QA, axis=2)
        pos_w = t0 - WIN_A + jnp.arange(WIN_A + QA)
        dist_w = t[:, None] - pos_w[None, :]
        mask_w = (dist_w >= 0) & (dist_w < WIN_A) & (pos_w[None, :] >= 0)
        bias_w = tbl[:, t5_bucket(dist_w)].reshape(G, H, QA, WIN_A + QA)
        s_w = jnp.einsum('bghqd,bgkd->bghqk', qi, k_wi).astype(jnp.float32) * SCALE + bias_w
        p_w = masked_softmax(s_w, mask_w)
        o_w = jnp.einsum('bghqk,bgkd->bghqd', p_w, v_wi)
        g = lax.dynamic_slice_in_dim(gates, t0, QA, axis=3)
        return g[..., 0:1] * o_c + g[..., 1:2] * o_s + g[..., 2:3] * o_w

    out = lax.map(block, jnp.arange(S // QA))
    return out.transpose(1, 0, 4, 2, 3, 5).reshape(B, S, WIDTH_A).astype(q.dtype)


def dilated_branch(q, k, v, tbl, window, dil):
    B, S, H, dh = q.shape
    L = S // dil
    band = window // dil
    nb = -(-L // band)
    Lp = nb * band

    def to_blocks(a):
        a = a.reshape(B, L, dil, H, dh).transpose(0, 2, 3, 1, 4)
        a = jnp.pad(a, ((0, 0), (0, 0), (0, 0), (0, Lp - L), (0, 0)))
        return a.reshape(B, dil, H, nb, band, dh)

    def with_prev(a):
        prev = jnp.pad(a[:, :, :, :-1], ((0, 0), (0, 0), (0, 0), (1, 0), (0, 0), (0, 0)))
        return jnp.concatenate([prev, a], axis=4)

    qb = to_blocks(q)
    kk = with_prev(to_blocks(k))
    vv = with_prev(to_blocks(v))
    a_ix = jnp.arange(band)[:, None]
    c_ix = jnp.arange(2 * band)[None, :]
    n = band + a_ix - c_ix
    key_idx = (jnp.arange(nb)[:, None, None] - 1) * band + c_ix[None]
    mask = (n >= 0) & (n <= band) & (key_idx >= 0)
    bias = tbl[:, t5_bucket(n * dil)]
    s = jnp.einsum('brhnqd,brhnkd->brhnqk', qb, kk).astype(jnp.float32) * SCALE + bias[:, None]
    s = jnp.where(mask, s, NEG)
    m = jnp.max(s, axis=-1, keepdims=True)
    e = jnp.exp(s - m)
    den = jnp.sum(e, axis=-1)
    o = jnp.einsum('brhnqk,brhnkd->brhnqd', e, vv) / den[..., None]
    lse = m[..., 0] + jnp.log(den)
    o = o.reshape(B, dil, H, Lp, dh)[:, :, :, :L].transpose(0, 3, 1, 2, 4).reshape(B, S, H, dh)
    lse = lse.reshape(B, dil, H, Lp)[..., :L].transpose(0, 3, 1, 2).reshape(B, S, H)
    return o, lse


def dilated_mixer(q, k, v, q_norm, k_norm, tbl):
    B, S = q.shape[0], q.shape[1]
    qn = rmsnorm(q, q_norm)
    kn = rmsnorm(k, k_norm)
    outs, lses = [], []
    for window, dil in DIL_PAIRS:
        o, lse = dilated_branch(qn, kn, v, tbl, window, dil)
        outs.append(o)
        lses.append(lse)
    w = jax.nn.softmax(jnp.stack(lses), axis=0)
    o = jnp.einsum('ibsh,ibshd->bshd', w, jnp.stack(outs))
    return o.reshape(B, S, WIDTH_B).astype(q.dtype)


def moe_swiglu(h, router_w, w_gate, w_up, w_down):
    B, S, D = h.shape
    hf = h.reshape(-1, D)
    N = hf.shape[0]
    NK = N * TOP_K
    logits = (hf @ router_w).astype(jnp.float32)
    top_logit, top_e = lax.top_k(logits, TOP_K)
    gate = jax.nn.softmax(top_logit, axis=-1)
    e_flat = top_e.reshape(-1)
    tok_flat = jnp.arange(NK) // TOP_K
    order = jnp.argsort(e_flat)
    e_sorted = e_flat[order]
    counts = jnp.bincount(e_flat, length=N_EXPERTS)
    padded = (counts + MOE_CHUNK - 1) // MOE_CHUNK * MOE_CHUNK
    start = jnp.cumsum(counts) - counts
    pend = jnp.cumsum(padded)
    pstart = pend - padded
    dest = pstart[e_sorted] + jnp.arange(NK) - start[e_sorted]
    n_chunks = -(-NK // MOE_CHUNK) + N_EXPERTS
    rows = n_chunks * MOE_CHUNK
    row_tok = jnp.zeros((rows,), jnp.int32).at[dest].set(tok_flat[order])
    row_gate = jnp.zeros((rows,), jnp.float32).at[dest].set(gate.reshape(-1)[order])
    chunk_e = jnp.minimum(jnp.sum(jnp.arange(n_chunks)[:, None] * MOE_CHUNK >= pend[None, :], axis=1), N_EXPERTS - 1)

    def run_chunk(args):
        tok, e = args
        return swiglu(hf[tok], w_gate[e], w_up[e], w_down[e])

    y = lax.map(run_chunk, (row_tok.reshape(n_chunks, MOE_CHUNK), chunk_e))
    out = jnp.zeros((N, D), jnp.float32).at[row_tok].add(y.reshape(rows, D).astype(jnp.float32) * row_gate[:, None])
    return out.astype(h.dtype).reshape(B, S, D)


def setup_inputs(seed: int = 0) -> dict:
    key = jax.random.key(seed)
    k = jax.random.split(key, 24)
    n_dense = (DEPTH + 1) // 2
    n_moe = DEPTH // 2

    def nrm(kk, shape, scale):
        return jax.random.normal(kk, shape, jnp.float32) * scale

    def gain(kk, shape):
        return 1.0 + 0.02 * jax.random.normal(kk, shape, jnp.float32)

    return {
        'x': nrm(k[0], (BATCH, SEQ, D_MODEL), 1.0),
        'rel_bias': nrm(k[1], (N_BUCKETS, N_HEADS), 0.5),
        'attn_norm': gain(k[2], (DEPTH, D_MODEL)),
        'w_in': nrm(k[3], (DEPTH, D_MODEL, IN_WIDTH), D_MODEL ** -0.5),
        'nsa_q_norm': gain(k[4], (DEPTH, HEAD_DIM)),
        'nsa_k_norm': gain(k[5], (DEPTH, 3, HEAD_DIM)),
        'cmp_pos': nrm(k[6], (DEPTH, 2, CMP_LEN, HEAD_DIM), 0.1),
        'cmp_w1': nrm(k[7], (DEPTH, 2, CMP_LEN * HEAD_DIM, CMP_HIDDEN), (CMP_LEN * HEAD_DIM) ** -0.5),
        'cmp_b1': nrm(k[8], (DEPTH, 2, CMP_HIDDEN), 0.01),
        'cmp_w2': nrm(k[9], (DEPTH, 2, CMP_HIDDEN, HEAD_DIM), CMP_HIDDEN ** -0.5),
        'dil_q_norm': gain(k[10], (DEPTH, HEAD_DIM)),
        'dil_k_norm': gain(k[11], (DEPTH, HEAD_DIM)),
        'out_norm': gain(k[12], (DEPTH, MIX_WIDTH)),
        'w_out': nrm(k[13], (DEPTH, MIX_WIDTH, D_MODEL), MIX_WIDTH ** -0.5),
        'ffn_norm': gain(k[14], (DEPTH, D_MODEL)),
        'ffn_w_gate': nrm(k[15], (n_dense, D_MODEL, D_FF), D_MODEL ** -0.5),
        'ffn_w_up': nrm(k[16], (n_dense, D_MODEL, D_FF), D_MODEL ** -0.5),
        'ffn_w_down': nrm(k[17], (n_dense, D_FF, D_MODEL), D_FF ** -0.5),
        'router_w': nrm(k[18], (n_moe, D_MODEL, N_EXPERTS), D_MODEL ** -0.5),
        'exp_w_gate': nrm(k[19], (n_moe, N_EXPERTS, D_MODEL, D_FF_EXPERT), D_MODEL ** -0.5),
        'exp_w_up': nrm(k[20], (n_moe, N_EXPERTS, D_MODEL, D_FF_EXPERT), D_MODEL ** -0.5),
        'exp_w_down': nrm(k[21], (n_moe, N_EXPERTS, D_FF_EXPERT, D_MODEL), D_FF_EXPERT ** -0.5),
    }


def reference(x, rel_bias, attn_norm, w_in, nsa_q_norm, nsa_k_norm, cmp_pos, cmp_w1, cmp_b1, cmp_w2,
              dil_q_norm, dil_k_norm, out_norm, w_out, ffn_norm, ffn_w_gate, ffn_w_up, ffn_w_down,
              router_w, exp_w_gate, exp_w_up, exp_w_down):
    B, S, _ = x.shape
    tbl = rel_bias.astype(jnp.float32).T
    tbl_a = tbl[:N_HEADS_A]
    tbl_b = tbl[N_HEADS_A:]
    split_points = [int(v) for v in np.cumsum(IN_SPLITS)[:-1]]

    def heads(a):
        return a.reshape(B, S, -1, HEAD_DIM)

    h = x
    for layer in range(DEPTH):
        u = rmsnorm(h, attn_norm[layer])
        proj = u @ w_in[layer]
        qa, kca, vca, ksa, vsa, kwa, vwa, ga, qb, kb, vb = jnp.split(proj, split_points, axis=-1)
        o_a = nsa_mixer(heads(qa), heads(kca), heads(vca), heads(ksa), heads(vsa), heads(kwa), heads(vwa), ga,
                        nsa_q_norm[layer], nsa_k_norm[layer], cmp_pos[layer], cmp_w1[layer], cmp_b1[layer],
                        cmp_w2[layer], tbl_a)
        o_b = dilated_mixer(heads(qb), heads(kb), heads(vb), dil_q_norm[layer], dil_k_norm[layer], tbl_b)
        o = jnp.concatenate([rmsnorm(o_a, out_norm[layer, :WIDTH_A]), rmsnorm(o_b, out_norm[layer, WIDTH_A:])], axis=-1)
        h = h + o @ w_out[layer]
        v = rmsnorm(h, ffn_norm[layer])
        if layer % 2 == 0:
            h = h + swiglu(v, ffn_w_gate[layer // 2], ffn_w_up[layer // 2], ffn_w_down[layer // 2])
        else:
            h = h + moe_swiglu(v, router_w[layer // 2], exp_w_gate[layer // 2], exp_w_up[layer // 2], exp_w_down[layer // 2])
    return h
```

```python
import functools
import math

import jax
import jax.numpy as jnp
import numpy as np
from jax import lax
from jax.experimental import pallas as pl
from jax.experimental.pallas import tpu as pltpu

F32 = jnp.float32
BF16 = jnp.bfloat16

D_MODEL = 1024
HEAD_DIM = 64
N_HEADS_A = 8
N_KV_A = 2
HPG_A = N_HEADS_A // N_KV_A
N_HEADS_B = 8
WIDTH_A = N_HEADS_A * HEAD_DIM
WIDTH_B = N_HEADS_B * HEAD_DIM
KV_A = N_KV_A * HEAD_DIM
CMP_LEN = 32
CMP_STRIDE = 16
CMP_HIDDEN = 256
SEL_BLOCK = 64
SEL_TOPK = 16
WIN_A = 512
DIL_PAIRS = ((128, 1), (512, 4), (2048, 16))
N_BUCKETS = 32
MAX_DISTANCE = 2048
N_EXPERTS = 8
TOP_K = 2
RMS_EPS = 1e-6
NEG = -1e30
SCALE = HEAD_DIM ** -0.5

LANES = 128
MXU_DIM = 256
VMEM_LIMIT = 56 * 1024 * 1024

TM_PROJ = 512
T_ATT = 128
DIL_BAND = 128
TM_FFN = 512
TF_FFN = 1408
TM_MOE = 512
TF_MOE = 512
DISPATCH_TOKENS = 1024
COMBINE_TOKENS = 256

COL_QA = 0
COL_CVA = 512
COL_KSW = 768
COL_QB = 1280
COL_KB = 1792
COL_VB = 2304
COL_GA = 2816
IN_COLS = 2944


def _cparams(sem, vmem=VMEM_LIMIT):
    return pltpu.CompilerParams(dimension_semantics=sem, vmem_limit_bytes=vmem)


def _dot(a, b):
    return jnp.dot(a, b, preferred_element_type=F32)


def _dot_nt(a, b):
    return lax.dot_general(a, b, (((1,), (1,)), ((), ())), preferred_element_type=F32)


def _split_bf16(x):
    hi = x.astype(BF16)
    lo = (x - hi.astype(F32)).astype(BF16)
    return hi, lo


def _bucket_thresholds():
    d = np.arange(0, 4 * MAX_DISTANCE, dtype=np.int64)
    max_exact = N_BUCKETS // 2
    scaled = np.log(np.maximum(d, 1).astype(np.float32) / np.float32(max_exact)) / np.float32(
        math.log(MAX_DISTANCE / max_exact))
    large = np.minimum(max_exact + (scaled.astype(np.float32) * (N_BUCKETS - max_exact)).astype(np.int32),
                       N_BUCKETS - 1)
    bucket = np.where(d < max_exact, d, large)
    assert np.all(np.diff(bucket) >= 0)
    return [int(np.argmax(bucket >= b)) for b in range(N_BUCKETS)]


_THR = _bucket_thresholds()
FAR_DIST = _THR[N_BUCKETS - 1]


def _bias_table_kernel(tbl_ref, out_ref, *, n_heads, rows, cols, dist_valid):
    i = pl.program_id(0)
    a = lax.broadcasted_iota(jnp.int32, (rows, cols), 0)
    c = lax.broadcasted_iota(jnp.int32, (rows, cols), 1)
    d, valid = dist_valid(i, a, c)
    for h in range(n_heads):
        acc = jnp.full((rows, cols), tbl_ref[h, 0], F32)
        for b in range(1, N_BUCKETS):
            acc = jnp.where(d >= _THR[b], tbl_ref[h, b], acc)
        out_ref[h, 0] = jnp.where(valid, acc, NEG)


def _bias_table(tbl, n_tiles, rows, cols, dist_valid):
    n_heads = tbl.shape[0]
    return pl.pallas_call(
        functools.partial(_bias_table_kernel, n_heads=n_heads, rows=rows, cols=cols, dist_valid=dist_valid),
        grid=(n_tiles,),
        in_specs=[pl.BlockSpec(memory_space=pltpu.SMEM)],
        out_specs=pl.BlockSpec((n_heads, 1, rows, cols), lambda i: (0, i, 0, 0)),
        out_shape=jax.ShapeDtypeStruct((n_heads, n_tiles, rows, cols), F32),
        compiler_params=_cparams(("arbitrary",)),
        name="bias_table",
    )(tbl)


def _cmp_dist(i, a, c, *, n_cmp):
    d = i * T_ATT + a - (c * CMP_STRIDE + CMP_LEN - 1)
    return d, (d >= 0) & (c < n_cmp)


def _sel_dist(i, a, c):
    d = i * T_ATT + a - c
    return d, d >= 0


def _win_dist(i, a, c):
    d = i * T_ATT + a - c
    return d, (d >= 0) & (d < WIN_A)


def _dil_dist(i, a, c, *, dil):
    n = DIL_BAND + a - c
    return n * dil, (n >= 0) & (n <= DIL_BAND)


def _inproj_kernel(h_ref, gn_ref, w_ref, gain_ref, bd_ref,
                   qa_ref, cva_ref, ksw_ref, qb_ref, kb_ref, vb_ref, ga_ref):
    x = h_ref[...]
    u = (x * lax.rsqrt(jnp.mean(x * x, axis=-1, keepdims=True) + RMS_EPS) * gn_ref[...]).astype(BF16)

    def proj(c0, width):
        return _dot(u, w_ref[:, c0:c0 + width])

    def headnorm(acc, c0):
        outs = []
        for j in range(acc.shape[1] // MXU_DIM):
            a = acc[:, j * MXU_DIM:(j + 1) * MXU_DIM]
            sq_hi, sq_lo = _split_bf16(a * a)
            ms = _dot(sq_hi, bd_ref[...]) + _dot(sq_lo, bd_ref[...])
            g = gain_ref[:, c0 + j * MXU_DIM:c0 + (j + 1) * MXU_DIM]
            outs.append(a * lax.rsqrt(ms + RMS_EPS) * g)
        return outs[0] if len(outs) == 1 else jnp.concatenate(outs, axis=1)

    qa_ref[...] = headnorm(proj(COL_QA, WIDTH_A), COL_QA).astype(BF16)
    cva_ref[...] = proj(COL_CVA, 2 * KV_A)
    ksw = proj(COL_KSW, 4 * KV_A)
    ksw_ref[:, :2 * KV_A] = headnorm(ksw[:, :2 * KV_A], COL_KSW).astype(BF16)
    ksw_ref[:, 2 * KV_A:] = ksw[:, 2 * KV_A:].astype(BF16)
    qb_ref[...] = headnorm(proj(COL_QB, WIDTH_B), COL_QB).astype(BF16)
    kb_ref[...] = headnorm(proj(COL_KB, WIDTH_B), COL_KB).astype(BF16)
    vb_ref[...] = proj(COL_VB, WIDTH_B).astype(BF16)
    ga_ref[...] = jax.nn.sigmoid(proj(COL_GA, LANES))


def _in_projection(h, attn_norm, w_in, nsa_q_norm, nsa_k_norm, dil_q_norm, dil_k_norm):
    n = h.shape[0]
    o = np.cumsum((0, WIDTH_A, KV_A, KV_A, KV_A, KV_A, KV_A, KV_A, N_HEADS_A * 3, WIDTH_B, WIDTH_B, WIDTH_B))
    seg = [w_in[:, o[i]:o[i + 1]] for i in range(11)]
    qa, kc, vc, ks, vs, kw, vw, ga, qb, kb, vb = seg
    pad = jnp.zeros((D_MODEL, IN_COLS - COL_GA - N_HEADS_A * 3), w_in.dtype)
    w = jnp.concatenate([qa, kc, vc, ks, kw, vs, vw, qb, kb, vb, ga, pad], axis=1).astype(BF16)
    ones = jnp.ones((IN_COLS,), F32)
    gain = ones
    gain = gain.at[COL_QA:COL_QA + WIDTH_A].set(jnp.tile(nsa_q_norm, N_HEADS_A) * SCALE)
    gain = gain.at[COL_KSW:COL_KSW + KV_A].set(jnp.tile(nsa_k_norm[1], N_KV_A))
    gain = gain.at[COL_KSW + KV_A:COL_KSW + 2 * KV_A].set(jnp.tile(nsa_k_norm[2], N_KV_A))
    gain = gain.at[COL_QB:COL_QB + WIDTH_B].set(jnp.tile(dil_q_norm, N_HEADS_B) * SCALE)
    gain = gain.at[COL_KB:COL_KB + WIDTH_B].set(jnp.tile(dil_k_norm, N_HEADS_B))
    blk = np.arange(MXU_DIM) // HEAD_DIM
    bd = jnp.asarray((blk[:, None] == blk[None, :]).astype(np.float32) / HEAD_DIM, BF16)

    tm = TM_PROJ
    row = lambda width: pl.BlockSpec((tm, width), lambda i: (i, 0))
    full = lambda a: pl.BlockSpec(a.shape, lambda i: (0,) * a.ndim)
    gn = attn_norm.reshape(1, D_MODEL)
    gain = gain.reshape(1, IN_COLS)
    outs = pl.pallas_call(
        _inproj_kernel,
        grid=(n // tm,),
        in_specs=[row(D_MODEL), full(gn), full(w), full(gain), full(bd)],
        out_specs=[row(WIDTH_A), row(2 * KV_A), row(4 * KV_A), row(WIDTH_B), row(WIDTH_B), row(WIDTH_B),
                   row(LANES)],
        out_shape=[jax.ShapeDtypeStruct((n, WIDTH_A), BF16), jax.ShapeDtypeStruct((n, 2 * KV_A), F32),
                   jax.ShapeDtypeStruct((n, 4 * KV_A), BF16), jax.ShapeDtypeStruct((n, WIDTH_B), BF16),
                   jax.ShapeDtypeStruct((n, WIDTH_B), BF16), jax.ShapeDtypeStruct((n, WIDTH_B), BF16),
                   jax.ShapeDtypeStruct((n, LANES), F32)],
        compiler_params=_cparams(("parallel",)),
        name="in_projection",
    )(h, gn, w, gain, bd)
    return outs


def _gelu_tanh(x):
    return 0.5 * x * (1.0 + jnp.tanh(math.sqrt(2.0 / math.pi) * (x + 0.044715 * (x * x * x))))


def _compress_kernel(x_ref, pos_ref, w1_ref, b1_ref, w2_ref, kg_ref, kc_ref, vc_ref):
    rows = x_ref.shape[1]
    half = CMP_LEN // 2
    for which, out_ref in ((0, kc_ref), (1, vc_ref)):
        top = jnp.zeros((rows, 2 * CMP_HIDDEN), F32)
        bot = jnp.zeros((rows, 2 * CMP_HIDDEN), F32)
        for l in range(half):
            c0 = l * 2 * KV_A + which * KV_A
            a = x_ref[0, :, c0:c0 + KV_A]
            top += _dot((a + pos_ref[which, l:l + 1, :]).astype(BF16), w1_ref[which, l])
            bot += _dot((a + pos_ref[which, half + l:half + l + 1, :]).astype(BF16), w1_ref[which, half + l])
        hid = top + pltpu.roll(bot, rows - 1, axis=0) + b1_ref[which]
        y = _dot(_gelu_tanh(hid).astype(BF16), w2_ref[which])
        if which == 0:
            parts = []
            for g in range(N_KV_A):
                yg = y[:, g * HEAD_DIM:(g + 1) * HEAD_DIM]
                parts.append(yg * lax.rsqrt(jnp.mean(yg * yg, axis=-1, keepdims=True) + RMS_EPS))
            y = jnp.concatenate(parts, axis=1) * kg_ref[...]
        out_ref[0] = y.astype(BF16)


def _compress(cva, batch, seq, cmp_pos, cmp_w1, cmp_b1, cmp_w2, k_norm0):
    rows = seq // CMP_STRIDE
    x = cva.reshape(batch, rows, CMP_STRIDE * 2 * KV_A)
    pos = jnp.tile(cmp_pos, (1, 1, N_KV_A))
    w1 = cmp_w1.reshape(2, CMP_LEN, HEAD_DIM, CMP_HIDDEN).astype(BF16)
    z1 = jnp.zeros_like(w1)
    w1 = jnp.concatenate([jnp.concatenate([w1, z1], axis=3), jnp.concatenate([z1, w1], axis=3)], axis=2)
    b1 = jnp.tile(cmp_b1, (1, N_KV_A)).reshape(2, 1, 2 * CMP_HIDDEN)
    w2 = cmp_w2.astype(BF16)
    z2 = jnp.zeros_like(w2)
    w2 = jnp.concatenate([jnp.concatenate([w2, z2], axis=2), jnp.concatenate([z2, w2], axis=2)], axis=1)
    kg = jnp.tile(k_norm0, N_KV_A).reshape(1, KV_A)
    full = lambda a: pl.BlockSpec(a.shape, lambda b: (0,) * a.ndim)
    out = pl.BlockSpec((1, rows, KV_A), lambda b: (b, 0, 0))
    return pl.pallas_call(
        _compress_kernel,
        grid=(batch,),
        in_specs=[pl.BlockSpec((1, rows, x.shape[2]), lambda b: (b, 0, 0)),
                  full(pos), full(w1), full(b1), full(w2), full(kg)],
        out_specs=[out, out],
        out_shape=[jax.ShapeDtypeStruct((batch, rows, KV_A), BF16)] * 2,
        compiler_params=_cparams(("parallel",)),
        name="nsa_compress",
    )(x, pos, w1, b1, w2, kg)


def _flash_tiles(q4, k_ref, k_col, v_col, bias_fn, key_tile_fn, n_tiles, extra_fn):
    t = T_ATT
    rows = q4.shape[0]

    def body(j, carry):
        m, l, acc = carry
        kt = key_tile_fn(j)
        r0 = pl.multiple_of(kt * t, t)
        k = k_ref[pl.ds(r0, t), k_col:k_col + HEAD_DIM]
        v = k_ref[pl.ds(r0, t), v_col:v_col + HEAD_DIM]
        s = _dot_nt(q4, k).reshape(HPG_A, t, t) + bias_fn(j)
        if extra_fn is not None:
            s = s + extra_fn(kt)[None]
        s = s.reshape(rows, t)
        m_new = jnp.maximum(m, jnp.max(s, axis=-1, keepdims=True))
        alpha = jnp.exp(m - m_new)
        p = jnp.exp(s - m_new)
        l = alpha * l + jnp.sum(p, axis=-1, keepdims=True)
        acc = alpha * acc + _dot(p.astype(BF16), v)
        return m_new, l, acc

    init = (jnp.full((rows, 1), NEG, F32), jnp.zeros((rows, 1), F32), jnp.zeros((rows, HEAD_DIM), F32))
    _, l, acc = lax.fori_loop(0, n_tiles, body, init)
    return acc / l


def _nsa_kernel(q_ref, kc_ref, vc_ref, ksw_ref, ga_ref, bc_ref, bs_ref, bw_ref, o_ref,
                imp_ref, mask_ref, *, seq):
    t = T_ATT
    qi = pl.program_id(1)
    n_cmp_rows = seq // CMP_STRIDE
    n_sb = seq // SEL_BLOCK
    n_kt = seq // t
    nd_sel = bs_ref.shape[1]
    nd_win = bw_ref.shape[1]
    k_sel = min(SEL_TOPK, n_sb)

    jj = lax.broadcasted_iota(jnp.int32, (n_sb, n_cmp_rows), 0) * SEL_BLOCK
    nn = lax.broadcasted_iota(jnp.int32, (n_sb, n_cmp_rows), 1) * CMP_STRIDE
    ov = jnp.maximum(jnp.minimum(nn + CMP_LEN, jj + SEL_BLOCK) - jnp.maximum(nn, jj), 0)
    ov_t = (ov.astype(F32) * (1.0 / CMP_LEN)).astype(BF16)
    eb = lax.broadcasted_iota(jnp.int32, (n_sb, seq), 0)
    ep = lax.broadcasted_iota(jnp.int32, (n_sb, seq), 1) // SEL_BLOCK
    expand = jnp.where(eb == ep, 1.0, 0.0).astype(BF16)

    blk = lax.broadcasted_iota(jnp.int32, (n_sb, t), 0)
    tpos = qi * t + lax.broadcasted_iota(jnp.int32, (n_sb, t), 1)
    cur = tpos // SEL_BLOCK
    forced = (blk == 0) | (blk == cur) | (blk == cur - 1)
    future = blk * SEL_BLOCK > tpos

    for g in range(N_KV_A):
        q4 = jnp.concatenate(
            [q_ref[:, (g * HPG_A + h) * HEAD_DIM:(g * HPG_A + h + 1) * HEAD_DIM] for h in range(HPG_A)], axis=0)

        kc = kc_ref[0, :, g * HEAD_DIM:(g + 1) * HEAD_DIM]
        vc = vc_ref[0, :, g * HEAD_DIM:(g + 1) * HEAD_DIM]
        s = _dot_nt(q4, kc).reshape(HPG_A, t, n_cmp_rows) + bc_ref[g * HPG_A:(g + 1) * HPG_A, 0]
        m = jnp.max(s, axis=-1, keepdims=True)
        e = jnp.where(s > 0.5 * NEG, jnp.exp(s - m), 0.0)
        l = jnp.sum(e, axis=-1, keepdims=True)
        p = e * (1.0 / jnp.maximum(l, 1e-30))
        o_c = _dot(p.reshape(HPG_A * t, n_cmp_rows).astype(BF16), vc)

        p_sum = p[0] + p[1] + p[2] + p[3]
        p_hi, p_lo = _split_bf16(p_sum)
        p_lo2 = (p_sum - p_hi.astype(F32) - p_lo.astype(F32)).astype(BF16)
        imp = _dot_nt(ov_t, p_hi) + _dot_nt(ov_t, p_lo) + _dot_nt(ov_t, p_lo2)
        imp = jnp.where(forced, 1e6, jnp.where(future, -1e6, imp))
        imp_ref[...] = imp
        cnt = jnp.zeros((n_sb, t), F32)
        for i in range(n_sb):
            ri = jnp.broadcast_to(imp_ref[pl.ds(i, 1), :], (n_sb, t))
            later = jnp.where(blk > i, 1.0, 0.0)
            cnt = cnt + jnp.where(ri > imp, 1.0, jnp.where(ri == imp, later, 0.0))
        sel = jnp.where(cnt < k_sel, 1.0, 0.0)
        sel_qk = _dot(sel.T.astype(BF16), expand)
        madd = (sel_qk - 1.0) * (-NEG)
        for kt in range(n_kt):
            mask_ref[kt] = madd[:, kt * t:(kt + 1) * t]

        o_s = _flash_tiles(
            q4, ksw_ref, g * HEAD_DIM, 2 * KV_A + g * HEAD_DIM,
            lambda j: bs_ref[g * HPG_A:(g + 1) * HPG_A, jnp.minimum(qi - j, nd_sel - 1)],
            lambda j: j, qi + 1, lambda kt: mask_ref[kt])

        o_w = _flash_tiles(
            q4, ksw_ref, KV_A + g * HEAD_DIM, 3 * KV_A + g * HEAD_DIM,
            lambda j: bw_ref[g * HPG_A:(g + 1) * HPG_A, j],
            lambda j: qi - j, jnp.minimum(qi, nd_win - 1) + 1, None)

        for h in range(HPG_A):
            hh = g * HPG_A + h
            gate = ga_ref[:, 3 * hh:3 * hh + 3]
            rows = slice(h * t, (h + 1) * t)
            o_ref[:, hh * HEAD_DIM:(hh + 1) * HEAD_DIM] = (
                gate[:, 0:1] * o_c[rows] + gate[:, 1:2] * o_s[rows] + gate[:, 2:3] * o_w[rows])


def _nsa_attention(qa, kc, vc, ksw, ga, bias_c, bias_s, bias_w, batch, seq):
    t = T_ATT
    nq = seq // t
    n = batch * seq
    rows_c = seq // CMP_STRIDE
    resident = lambda a: pl.BlockSpec(a.shape, lambda b, i: (0,) * a.ndim, pipeline_mode=pl.Buffered(1))
    return pl.pallas_call(
        functools.partial(_nsa_kernel, seq=seq),
        grid=(batch, nq),
        in_specs=[
            pl.BlockSpec((t, WIDTH_A), lambda b, i: (b * nq + i, 0)),
            pl.BlockSpec((1, rows_c, KV_A), lambda b, i: (b, 0, 0)),
            pl.BlockSpec((1, rows_c, KV_A), lambda b, i: (b, 0, 0)),
            pl.BlockSpec((seq, 4 * KV_A), lambda b, i: (b, 0)),
            pl.BlockSpec((t, LANES), lambda b, i: (b * nq + i, 0)),
            pl.BlockSpec((N_HEADS_A, 1, t, rows_c), lambda b, i: (0, i, 0, 0)),
            resident(bias_s),
            resident(bias_w),
        ],
        out_specs=pl.BlockSpec((t, WIDTH_A), lambda b, i: (b * nq + i, 0)),
        out_shape=jax.ShapeDtypeStruct((n, WIDTH_A), F32),
        scratch_shapes=[pltpu.VMEM((seq // SEL_BLOCK, t), F32), pltpu.VMEM((seq // t, t, t), F32)],
        compiler_params=_cparams(("parallel", "arbitrary")),
        name="nsa_attention",
    )(qa, kc, vc, ksw, ga, bias_c, bias_s, bias_w)


def _dilated_kernel(q_ref, kp_ref, kc_ref, vp_ref, vc_ref, bias_ref, o_ref, lse_ref):
    band = DIL_BAND
    first = pl.program_id(2) == 0
    prev_mask = jnp.where(first, NEG, 0.0)
    for h in range(N_HEADS_B):
        cols = slice(h * HEAD_DIM, (h + 1) * HEAD_DIM)
        q = q_ref[0, :, cols]
        s_p = _dot_nt(q, kp_ref[0, :, cols]) + bias_ref[h, 0, :, :band] + prev_mask
        s_c = _dot_nt(q, kc_ref[0, :, cols]) + bias_ref[h, 0, :, band:]
        m = jnp.maximum(jnp.max(s_p, axis=-1, keepdims=True), jnp.max(s_c, axis=-1, keepdims=True))
        e_p = jnp.exp(s_p - m)
        e_c = jnp.exp(s_c - m)
        den = jnp.sum(e_p, axis=-1, keepdims=True) + jnp.sum(e_c, axis=-1, keepdims=True)
        o = _dot(e_p.astype(BF16), vp_ref[0, :, cols]) + _dot(e_c.astype(BF16), vc_ref[0, :, cols])
        o_ref[0, :, cols] = o / den
        lse_ref[0, :, cols] = jnp.broadcast_to(m + jnp.log(den), (band, HEAD_DIM))


def _dilated_attention(qb, kb, vb, bias, batch, seq, dil):
    band = DIL_BAND
    length = seq // dil
    nb = length // band
    view = lambda a: a.reshape(batch, length, dil * WIDTH_B)
    cur = pl.BlockSpec((1, band, WIDTH_B), lambda b, r, n: (b, n, r))
    prev = pl.BlockSpec((1, band, WIDTH_B), lambda b, r, n: (b, jnp.maximum(n - 1, 0), r))
    o, lse = pl.pallas_call(
        _dilated_kernel,
        grid=(batch, dil, nb),
        in_specs=[cur, prev, cur, prev, cur,
                  pl.BlockSpec(bias.shape, lambda b, r, n: (0, 0, 0, 0))],
        out_specs=[cur, cur],
        out_shape=[jax.ShapeDtypeStruct((batch, length, dil * WIDTH_B), F32)] * 2,
        compiler_params=_cparams(("parallel", "parallel", "arbitrary")),
        name=f"dilated_attention_d{dil}",
    )(view(qb), view(kb), view(kb), view(vb), view(vb), bias)
    n = batch * seq
    return o.reshape(n, WIDTH_B), lse.reshape(n, WIDTH_B)


def _outproj_kernel(*refs, with_router):
    if with_router:
        (oa_ref, o1_ref, o2_ref, o3_ref, l1_ref, l2_ref, l3_ref, h_ref, gn_ref, w_ref, fg_ref, rw_ref,
         h_out_ref, v_ref, route_ref) = refs
    else:
        (oa_ref, o1_ref, o2_ref, o3_ref, l1_ref, l2_ref, l3_ref, h_ref, gn_ref, w_ref, fg_ref,
         h_out_ref, v_ref) = refs
    l1, l2, l3 = l1_ref[...], l2_ref[...], l3_ref[...]
    m = jnp.maximum(jnp.maximum(l1, l2), l3)
    e1, e2, e3 = jnp.exp(l1 - m), jnp.exp(l2 - m), jnp.exp(l3 - m)
    ob = (e1 * o1_ref[...] + e2 * o2_ref[...] + e3 * o3_ref[...]) / (e1 + e2 + e3)

    def norm(x, g):
        return x * lax.rsqrt(jnp.mean(x * x, axis=-1, keepdims=True) + RMS_EPS) * g

    o = jnp.concatenate([norm(oa_ref[...], gn_ref[:, :WIDTH_A]), norm(ob, gn_ref[:, WIDTH_A:])], axis=1)
    h = h_ref[...] + _dot(o.astype(BF16), w_ref[...])
    h_out_ref[...] = h
    v = norm(h, fg_ref[...])
    v_ref[...] = v.astype(v_ref.dtype)
    if with_router:
        v_hi, v_lo = _split_bf16(v)
        w_hi, w_lo = _split_bf16(rw_ref[...])
        logits = _dot(v_hi, w_hi) + _dot(v_lo, w_hi) + _dot(v_hi, w_lo)
        lane = lax.broadcasted_iota(jnp.int32, logits.shape, 1).astype(F32)
        lg = jnp.where(lane < N_EXPERTS, logits, -jnp.inf)
        m1 = jnp.max(lg, axis=-1, keepdims=True)
        i1 = jnp.min(jnp.where(lg == m1, lane, float(LANES)), axis=-1, keepdims=True)
        lg2 = jnp.where(lane == i1, -jnp.inf, lg)
        m2 = jnp.max(lg2, axis=-1, keepdims=True)
        i2 = jnp.min(jnp.where(lg2 == m2, lane, float(LANES)), axis=-1, keepdims=True)
        e = jnp.exp(m2 - m1)
        g1 = 1.0 / (1.0 + e)
        g2 = e / (1.0 + e)
        route_ref[...] = jnp.where(lane == 0, i1, jnp.where(
            lane == 1, i2, jnp.where(lane == 2, g1, jnp.where(lane == 3, g2, 0.0))))


def _out_projection(o_a, dil_outs, h, out_norm, w_out, ffn_norm, router_w):
    n = h.shape[0]
    tm = TM_PROJ
    with_router = router_w is not None
    row = lambda width: pl.BlockSpec((tm, width), lambda i: (i, 0))
    full = lambda a: pl.BlockSpec(a.shape, lambda i: (0,) * a.ndim)
    (o1, l1), (o2, l2), (o3, l3) = dil_outs
    gn = out_norm.reshape(1, -1)
    fg = ffn_norm.reshape(1, -1)
    w = w_out.astype(BF16)
    args = [o_a, o1, o2, o3, l1, l2, l3, h, gn, w, fg]
    in_specs = [row(WIDTH_A)] + [row(WIDTH_B)] * 6 + [row(D_MODEL), full(gn), full(w), full(fg)]
    out_specs = [row(D_MODEL), row(D_MODEL)]
    out_shape = [jax.ShapeDtypeStruct((n, D_MODEL), F32),
                 jax.ShapeDtypeStruct((n, D_MODEL), F32 if with_router else BF16)]
    if with_router:
        rw = jnp.pad(router_w, ((0, 0), (0, LANES - N_EXPERTS)))
        args.append(rw)
        in_specs.append(full(rw))
        out_specs.append(row(LANES))
        out_shape.append(jax.ShapeDtypeStruct((n, LANES), F32))
    return pl.pallas_call(
        functools.partial(_outproj_kernel, with_router=with_router),
        grid=(n // tm,),
        in_specs=in_specs,
        out_specs=out_specs,
        out_shape=out_shape,
        compiler_params=_cparams(("parallel",)),
        name="out_projection",
    )(*args)


def _ffn_kernel(v_ref, h_ref, wg_ref, wu_ref, wd_ref, o_ref, acc_ref):
    f = pl.program_id(1)

    @pl.when(f == 0)
    def _():
        acc_ref[...] = jnp.zeros_like(acc_ref)

    v = v_ref[...]
    gate = _dot(v, wg_ref[...])
    up = _dot(v, wu_ref[...])
    acc_ref[...] += _dot((jax.nn.silu(gate) * up).astype(BF16), wd_ref[...])

    @pl.when(f == pl.num_programs(1) - 1)
    def _():
        o_ref[...] = h_ref[...] + acc_ref[...]


def _dense_ffn(v, h, w_gate, w_up, w_down):
    n = h.shape[0]
    d_ff = w_gate.shape[1]
    tm, tf = TM_FFN, TF_FFN
    return pl.pallas_call(
        _ffn_kernel,
        grid=(n // tm, d_ff // tf),
        in_specs=[pl.BlockSpec((tm, D_MODEL), lambda i, f: (i, 0)),
                  pl.BlockSpec((tm, D_MODEL), lambda i, f: (i, 0)),
                  pl.BlockSpec((D_MODEL, tf), lambda i, f: (0, f)),
                  pl.BlockSpec((D_MODEL, tf), lambda i, f: (0, f)),
                  pl.BlockSpec((tf, D_MODEL), lambda i, f: (f, 0))],
        out_specs=pl.BlockSpec((tm, D_MODEL), lambda i, f: (i, 0)),
        out_shape=jax.ShapeDtypeStruct((n, D_MODEL), F32),
        scratch_shapes=[pltpu.VMEM((tm, D_MODEL), F32)],
        compiler_params=_cparams(("parallel", "arbitrary")),
        name="dense_ffn",
    )(v, h, w_gate.astype(BF16), w_up.astype(BF16), w_down.astype(BF16))


def _row_copy(src_ref, src_row, dst_ref, dst_row, sem):
    return pltpu.make_async_copy(src_ref.at[pl.ds(src_row, 1)], dst_ref.at[pl.ds(dst_row, 1)], sem)


def _dispatch_kernel(dest_ref, v_ref, xs_in_ref, xs_ref, sem):
    del xs_in_ref
    n_slots = dest_ref.shape[2]
    base = pl.program_id(0) * (n_slots // TOP_K)

    def issue(i, carry):
        _row_copy(v_ref, base + i // TOP_K, xs_ref, dest_ref[0, 0, i], sem).start()
        return carry

    lax.fori_loop(0, n_slots, issue, 0)
    pltpu.make_async_copy(v_ref.at[pl.ds(0, n_slots)], xs_ref.at[pl.ds(0, n_slots)], sem).wait()


def _dispatch(v, dest, n_rows):
    n = v.shape[0]
    slots = DISPATCH_TOKENS * TOP_K
    return pl.pallas_call(
        _dispatch_kernel,
        grid=(n // DISPATCH_TOKENS,),
        in_specs=[pl.BlockSpec((1, 1, slots), lambda i: (i, 0, 0), memory_space=pltpu.SMEM),
                  pl.BlockSpec(memory_space=pl.ANY),
                  pl.BlockSpec(memory_space=pl.ANY)],
        out_specs=pl.BlockSpec(memory_space=pl.ANY),
        out_shape=jax.ShapeDtypeStruct((n_rows, D_MODEL), F32),
        scratch_shapes=[pltpu.SemaphoreType.DMA(())],
        input_output_aliases={2: 0},
        compiler_params=_cparams(("arbitrary",)),
        name="moe_dispatch",
    )(dest.reshape(n // DISPATCH_TOKENS, 1, slots), v, jnp.zeros((n_rows, D_MODEL), F32))


def _gmm_kernel(te_ref, tv_ref, x_ref, wg_ref, wu_ref, wd_ref, y_ref, xb_ref, acc_ref):
    c = pl.program_id(0)
    f = pl.program_id(1)
    valid = tv_ref[c] > 0

    @pl.when(f == 0)
    def _():
        acc_ref[...] = jnp.zeros_like(acc_ref)
        xb_ref[...] = x_ref[...].astype(BF16)

    @pl.when(valid)
    def _():
        x = xb_ref[...]
        gate = _dot(x, wg_ref[0])
        up = _dot(x, wu_ref[0])
        acc_ref[...] += _dot((jax.nn.silu(gate) * up).astype(BF16), wd_ref[0])

    @pl.when(f == pl.num_programs(1) - 1)
    def _():
        y_ref[...] = acc_ref[...]


def _grouped_swiglu(xs, tile_expert, tile_valid, w_gate, w_up, w_down):
    n_rows = xs.shape[0]
    d_ff = w_gate.shape[2]
    tm, tf = TM_MOE, TF_MOE
    grid_spec = pltpu.PrefetchScalarGridSpec(
        num_scalar_prefetch=2,
        grid=(n_rows // tm, d_ff // tf),
        in_specs=[pl.BlockSpec((tm, D_MODEL), lambda c, f, te, tv: (c, 0)),
                  pl.BlockSpec((1, D_MODEL, tf), lambda c, f, te, tv: (te[c], 0, f)),
                  pl.BlockSpec((1, D_MODEL, tf), lambda c, f, te, tv: (te[c], 0, f)),
                  pl.BlockSpec((1, tf, D_MODEL), lambda c, f, te, tv: (te[c], f, 0))],
        out_specs=pl.BlockSpec((tm, D_MODEL), lambda c, f, te, tv: (c, 0)),
        scratch_shapes=[pltpu.VMEM((tm, D_MODEL), BF16), pltpu.VMEM((tm, D_MODEL), F32)],
    )
    return pl.pallas_call(
        _gmm_kernel,
        grid_spec=grid_spec,
        out_shape=jax.ShapeDtypeStruct((n_rows, D_MODEL), F32),
        compiler_params=_cparams(("parallel", "arbitrary")),
        name="moe_grouped_swiglu",
    )(tile_expert, tile_valid, xs, w_gate.astype(BF16), w_up.astype(BF16), w_down.astype(BF16))


def _combine_kernel(dest_ref, y_ref, route_ref, h_ref, o_ref, buf_ref, sem):
    tc = h_ref.shape[0]

    def issue(i, carry):
        for k in range(TOP_K):
            _row_copy(y_ref, dest_ref[0, 0, TOP_K * i + k], buf_ref.at[k], i, sem).start()
        return carry

    lax.fori_loop(0, tc, issue, 0)
    for k in range(TOP_K):
        pltpu.make_async_copy(y_ref.at[pl.ds(0, tc)], buf_ref.at[k], sem).wait()
    g = route_ref[...]
    o_ref[...] = h_ref[...] + g[:, 2:3] * buf_ref[0] + g[:, 3:4] * buf_ref[1]


def _combine(y, dest, route, h):
    n = h.shape[0]
    tc = COMBINE_TOKENS
    return pl.pallas_call(
        _combine_kernel,
        grid=(n // tc,),
        in_specs=[pl.BlockSpec((1, 1, tc * TOP_K), lambda i: (i, 0, 0), memory_space=pltpu.SMEM),
                  pl.BlockSpec(memory_space=pl.ANY),
                  pl.BlockSpec((tc, LANES), lambda i: (i, 0)),
                  pl.BlockSpec((tc, D_MODEL), lambda i: (i, 0))],
        out_specs=pl.BlockSpec((tc, D_MODEL), lambda i: (i, 0)),
        out_shape=jax.ShapeDtypeStruct((n, D_MODEL), F32),
        scratch_shapes=[pltpu.VMEM((TOP_K, tc, D_MODEL), F32), pltpu.SemaphoreType.DMA(())],
        compiler_params=_cparams(("arbitrary",)),
        name="moe_combine",
    )(dest.reshape(n // tc, 1, tc * TOP_K), y, route, h)


def _moe(v, route, h, w_gate, w_up, w_down):
    n = h.shape[0]
    nk = n * TOP_K
    tm = TM_MOE
    e_flat = route[:, :TOP_K].astype(jnp.int32).reshape(nk)
    onehot = (e_flat[:, None] == jnp.arange(N_EXPERTS)[None, :]).astype(jnp.int32)
    csum = jnp.cumsum(onehot, axis=0)
    counts = csum[-1]
    rank = jnp.sum(jnp.where(onehot > 0, csum, 0), axis=1) - 1
    padded = (counts + tm - 1) // tm * tm
    pend = jnp.cumsum(padded)
    pstart = pend - padded
    dest = (jnp.sum(jnp.where(onehot > 0, pstart[None, :], 0), axis=1) + rank).astype(jnp.int32)
    n_tiles = nk // tm + N_EXPERTS
    tile_start = jnp.arange(n_tiles) * tm
    tile_expert = jnp.minimum(jnp.sum(tile_start[:, None] >= pend[None, :], axis=1), N_EXPERTS - 1)
    tile_valid = (tile_start < pend[-1]).astype(jnp.int32)
    xs = _dispatch(v, dest, n_tiles * tm)
    y = _grouped_swiglu(xs, tile_expert.astype(jnp.int32), tile_valid, w_gate, w_up, w_down)
    return _combine(y, dest, route, h)


def kernel(x, rel_bias, attn_norm, w_in, nsa_q_norm, nsa_k_norm, cmp_pos, cmp_w1, cmp_b1, cmp_w2,
           dil_q_norm, dil_k_norm, out_norm, w_out, ffn_norm, ffn_w_gate, ffn_w_up, ffn_w_down,
           router_w, exp_w_gate, exp_w_up, exp_w_down):
    batch, seq, _ = x.shape
    depth = attn_norm.shape[0]
    n = batch * seq
    assert seq % (DIL_PAIRS[-1][1] * DIL_BAND) == 0 and n % DISPATCH_TOKENS == 0
    assert all(w == d * DIL_BAND for w, d in DIL_PAIRS)

    tbl = rel_bias.astype(F32).T
    tbl_a, tbl_b = tbl[:N_HEADS_A], tbl[N_HEADS_A:]
    t = T_ATT
    nd_sel = min(seq // t, -(-(FAR_DIST + t - 1) // t) + 1)
    bias_c = _bias_table(tbl_a, seq // t, t, seq // CMP_STRIDE,
                         functools.partial(_cmp_dist, n_cmp=(seq - CMP_LEN) // CMP_STRIDE + 1))
    bias_s = _bias_table(tbl_a, nd_sel, t, t, _sel_dist)
    bias_w = _bias_table(tbl_a, min(seq // t, WIN_A // t + 1), t, t, _win_dist)
    bias_d = [_bias_table(tbl_b, 1, DIL_BAND, 2 * DIL_BAND, functools.partial(_dil_dist, dil=d))
              for _, d in DIL_PAIRS]

    h = x.reshape(n, D_MODEL)
    for layer in range(depth):
        qa, cva, ksw, qb, kb, vb, ga = _in_projection(
            h, attn_norm[layer], w_in[layer], nsa_q_norm[layer], nsa_k_norm[layer],
            dil_q_norm[layer], dil_k_norm[layer])
        kc, vc = _compress(cva, batch, seq, cmp_pos[layer], cmp_w1[layer], cmp_b1[layer], cmp_w2[layer],
                           nsa_k_norm[layer, 0])
        o_a = _nsa_attention(qa, kc, vc, ksw, ga, bias_c, bias_s, bias_w, batch, seq)
        dil_outs = [_dilated_attention(qb, kb, vb, bias_d[i], batch, seq, d)
                    for i, (_, d) in enumerate(DIL_PAIRS)]
        moe_layer = layer % 2 == 1
        outs = _out_projection(o_a, dil_outs, h, out_norm[layer], w_out[layer], ffn_norm[layer],
                               router_w[layer // 2] if moe_layer else None)
        if moe_layer:
            h1, v, route = outs
            h = _moe(v, route, h1, exp_w_gate[layer // 2], exp_w_up[layer // 2], exp_w_down[layer // 2])
        else:
            h1, v = outs
            h = _dense_ffn(v, h1, ffn_w_gate[layer // 2], ffn_w_up[layer // 2], ffn_w_down[layer // 2])
    return h.reshape(batch, seq, D_MODEL)
```

```python
import functools
import math

import jax
import jax.numpy as jnp
import numpy as np
from jax import lax
from jax.experimental import pallas as pl
from jax.experimental.pallas import tpu as pltpu

F32 = jnp.float32
BF16 = jnp.bfloat16

D_MODEL = 1024
HEAD_DIM = 64
N_HEADS_A = 8
N_KV_A = 2
HPG_A = N_HEADS_A // N_KV_A
N_HEADS_B = 8
WIDTH_A = N_HEADS_A * HEAD_DIM
WIDTH_B = N_HEADS_B * HEAD_DIM
KV_A = N_KV_A * HEAD_DIM
CMP_LEN = 32
CMP_STRIDE = 16
CMP_HIDDEN = 256
SEL_BLOCK = 64
SEL_TOPK = 16
WIN_A = 512
DIL_PAIRS = ((128, 1), (512, 4), (2048, 16))
N_BUCKETS = 32
MAX_DISTANCE = 2048
N_EXPERTS = 8
TOP_K = 2
RMS_EPS = 1e-6
NEG = -1e30
SCALE = HEAD_DIM ** -0.5

LANES = 128
MXU_DIM = 256
VMEM_LIMIT = 56 * 1024 * 1024

TM_PROJ = 512
TQ_NSA = 256
TK_NSA = 512
TOEP = 128
DIL_BAND = 128
DIL_ROWS = 256
TM_FFN = 512
TF_FFN = 1408
TM_MOE = 512
TF_MOE = 512
DISPATCH_TOKENS = 512
COMBINE_TOKENS = 256

COL_QA = 0
COL_CVA = 512
COL_KSW = 768
COL_QB = 1280
COL_KB = 1792
COL_VB = 2304
COL_GA = 2816
IN_COLS = 2944
PAIRED_HEADS_A = np.array([h + g * HPG_A for h in range(HPG_A) for g in range(N_KV_A)])


def _cparams(sem, vmem=VMEM_LIMIT):
    return pltpu.CompilerParams(dimension_semantics=sem, vmem_limit_bytes=vmem)


def _dot(a, b):
    return jnp.dot(a, b, preferred_element_type=F32)


def _dot_nt(a, b):
    return lax.dot_general(a, b, (((1,), (1,)), ((), ())), preferred_element_type=F32)


def _split_bf16(x):
    hi = x.astype(BF16)
    lo = (x - hi.astype(F32)).astype(BF16)
    return hi, lo


def _bucket_thresholds():
    d = np.arange(0, 4 * MAX_DISTANCE, dtype=np.int64)
    max_exact = N_BUCKETS // 2
    scaled = np.log(np.maximum(d, 1).astype(np.float32) / np.float32(max_exact)) / np.float32(
        math.log(MAX_DISTANCE / max_exact))
    large = np.minimum(max_exact + (scaled.astype(np.float32) * (N_BUCKETS - max_exact)).astype(np.int32),
                       N_BUCKETS - 1)
    bucket = np.where(d < max_exact, d, large)
    assert np.all(np.diff(bucket) >= 0)
    return [int(np.argmax(bucket >= b)) for b in range(N_BUCKETS)]


_THR = _bucket_thresholds()
FAR_DIST = _THR[N_BUCKETS - 1]


def _bias_table_kernel(tbl_ref, out_ref, *, n_heads, rows, cols, dist_valid):
    i = pl.program_id(0)
    a = lax.broadcasted_iota(jnp.int32, (rows, cols), 0)
    c = lax.broadcasted_iota(jnp.int32, (rows, cols), 1)
    d, valid = dist_valid(i, a, c)
    for h in range(n_heads):
        acc = jnp.full((rows, cols), tbl_ref[h, 0], F32)
        for b in range(1, N_BUCKETS):
            acc = jnp.where(d >= _THR[b], tbl_ref[h, b], acc)
        out_ref[h, 0] = jnp.where(valid, acc, NEG)


def _bias_table(tbl, n_tiles, rows, cols, dist_valid):
    n_heads = tbl.shape[0]
    return pl.pallas_call(
        functools.partial(_bias_table_kernel, n_heads=n_heads, rows=rows, cols=cols, dist_valid=dist_valid),
        grid=(n_tiles,),
        in_specs=[pl.BlockSpec(memory_space=pltpu.SMEM)],
        out_specs=pl.BlockSpec((n_heads, 1, rows, cols), lambda i: (0, i, 0, 0)),
        out_shape=jax.ShapeDtypeStruct((n_heads, n_tiles, rows, cols), F32),
        compiler_params=_cparams(("arbitrary",)),
        name="bias_table",
    )(tbl)


def _cmp_dist(i, a, c, *, n_cmp):
    d = i * TQ_NSA + a - (c * CMP_STRIDE + CMP_LEN - 1)
    return d, (d >= 0) & (c < n_cmp)


def _sel_dist(i, a, c):
    d = (i - 1) * TOEP + a - c
    return d, d >= 0


def _win_dist(i, a, c):
    d = (i - 1) * TOEP + a - c
    return d, (d >= 0) & (d < WIN_A)


def _dil_dist(i, a, c, *, dil):
    n = DIL_BAND + a - c
    return n * dil, (n >= 0) & (n <= DIL_BAND)


def _inproj_kernel(h_ref, gn_ref, w_ref, gain_ref, bd_ref,
                   qa_ref, cva_ref, ksw_ref, qb_ref, kb_ref, vb_ref, ga_ref):
    x = h_ref[...]
    u = (x * lax.rsqrt(jnp.mean(x * x, axis=-1, keepdims=True) + RMS_EPS) * gn_ref[...]).astype(BF16)

    def proj(c0, width):
        return _dot(u, w_ref[:, c0:c0 + width])

    def headnorm(acc, c0):
        outs = []
        for j in range(acc.shape[1] // MXU_DIM):
            a = acc[:, j * MXU_DIM:(j + 1) * MXU_DIM]
            sq_hi, sq_lo = _split_bf16(a * a)
            ms = _dot(sq_hi, bd_ref[...]) + _dot(sq_lo, bd_ref[...])
            g = gain_ref[:, c0 + j * MXU_DIM:c0 + (j + 1) * MXU_DIM]
            outs.append(a * lax.rsqrt(ms + RMS_EPS) * g)
        return outs[0] if len(outs) == 1 else jnp.concatenate(outs, axis=1)

    qa_ref[...] = headnorm(proj(COL_QA, WIDTH_A), COL_QA).astype(BF16)
    cva_ref[...] = proj(COL_CVA, 2 * KV_A)
    ksw = proj(COL_KSW, 4 * KV_A)
    ksw_ref[:, :2 * KV_A] = headnorm(ksw[:, :2 * KV_A], COL_KSW).astype(BF16)
    ksw_ref[:, 2 * KV_A:] = ksw[:, 2 * KV_A:].astype(BF16)
    qb_ref[...] = headnorm(proj(COL_QB, WIDTH_B), COL_QB).astype(BF16)
    kb_ref[...] = headnorm(proj(COL_KB, WIDTH_B), COL_KB).astype(BF16)
    vb_ref[...] = proj(COL_VB, WIDTH_B).astype(BF16)
    ga_ref[...] = jax.nn.sigmoid(proj(COL_GA, LANES))


def _in_projection(h, attn_norm, w_in, nsa_q_norm, nsa_k_norm, dil_q_norm, dil_k_norm):
    n = h.shape[0]
    o = np.cumsum((0, WIDTH_A, KV_A, KV_A, KV_A, KV_A, KV_A, KV_A, N_HEADS_A * 3, WIDTH_B, WIDTH_B, WIDTH_B))
    seg = [w_in[:, o[i]:o[i + 1]] for i in range(11)]
    qa, kc, vc, ks, vs, kw, vw, ga, qb, kb, vb = seg
    qa = qa.reshape(D_MODEL, N_HEADS_A, HEAD_DIM)[:, PAIRED_HEADS_A, :].reshape(D_MODEL, WIDTH_A)
    pad = jnp.zeros((D_MODEL, IN_COLS - COL_GA - N_HEADS_A * 3), w_in.dtype)
    w = jnp.concatenate([qa, kc, vc, ks, kw, vs, vw, qb, kb, vb, ga, pad], axis=1).astype(BF16)
    ones = jnp.ones((IN_COLS,), F32)
    gain = ones
    gain = gain.at[COL_QA:COL_QA + WIDTH_A].set(jnp.tile(nsa_q_norm, N_HEADS_A) * SCALE)
    gain = gain.at[COL_KSW:COL_KSW + KV_A].set(jnp.tile(nsa_k_norm[1], N_KV_A))
    gain = gain.at[COL_KSW + KV_A:COL_KSW + 2 * KV_A].set(jnp.tile(nsa_k_norm[2], N_KV_A))
    gain = gain.at[COL_QB:COL_QB + WIDTH_B].set(jnp.tile(dil_q_norm, N_HEADS_B) * SCALE)
    gain = gain.at[COL_KB:COL_KB + WIDTH_B].set(jnp.tile(dil_k_norm, N_HEADS_B))
    blk = np.arange(MXU_DIM) // HEAD_DIM
    bd = jnp.asarray((blk[:, None] == blk[None, :]).astype(np.float32) / HEAD_DIM, BF16)

    tm = TM_PROJ
    row = lambda width: pl.BlockSpec((tm, width), lambda i: (i, 0))
    full = lambda a: pl.BlockSpec(a.shape, lambda i: (0,) * a.ndim)
    gn = attn_norm.reshape(1, D_MODEL)
    gain = gain.reshape(1, IN_COLS)
    outs = pl.pallas_call(
        _inproj_kernel,
        grid=(n // tm,),
        in_specs=[row(D_MODEL), full(gn), full(w), full(gain), full(bd)],
        out_specs=[row(WIDTH_A), row(2 * KV_A), row(4 * KV_A), row(WIDTH_B), row(WIDTH_B), row(WIDTH_B),
                   row(LANES)],
        out_shape=[jax.ShapeDtypeStruct((n, WIDTH_A), BF16), jax.ShapeDtypeStruct((n, 2 * KV_A), F32),
                   jax.ShapeDtypeStruct((n, 4 * KV_A), BF16), jax.ShapeDtypeStruct((n, WIDTH_B), BF16),
                   jax.ShapeDtypeStruct((n, WIDTH_B), BF16), jax.ShapeDtypeStruct((n, WIDTH_B), BF16),
                   jax.ShapeDtypeStruct((n, LANES), F32)],
        compiler_params=_cparams(("parallel",)),
        name="in_projection",
    )(h, gn, w, gain, bd)
    return outs


def _gelu_tanh(x):
    return 0.5 * x * (1.0 + jnp.tanh(math.sqrt(2.0 / math.pi) * (x + 0.044715 * (x * x * x))))


def _compress_kernel(x_ref, pos_ref, w1_ref, b1_ref, w2_ref, kg_ref, kc_ref, vc_ref):
    rows = x_ref.shape[1]
    half = CMP_LEN // 2
    for which, out_ref in ((0, kc_ref), (1, vc_ref)):
        top = jnp.zeros((rows, 2 * CMP_HIDDEN), F32)
        bot = jnp.zeros((rows, 2 * CMP_HIDDEN), F32)
        for l in range(half):
            c0 = l * 2 * KV_A + which * KV_A
            a = x_ref[0, :, c0:c0 + KV_A]
            top += _dot((a + pos_ref[which, l:l + 1, :]).astype(BF16), w1_ref[which, l])
            bot += _dot((a + pos_ref[which, half + l:half + l + 1, :]).astype(BF16), w1_ref[which, half + l])
        hid = top + pltpu.roll(bot, rows - 1, axis=0) + b1_ref[which]
        y = _dot(_gelu_tanh(hid).astype(BF16), w2_ref[which])
        if which == 0:
            parts = []
            for g in range(N_KV_A):
                yg = y[:, g * HEAD_DIM:(g + 1) * HEAD_DIM]
                parts.append(yg * lax.rsqrt(jnp.mean(yg * yg, axis=-1, keepdims=True) + RMS_EPS))
            y = jnp.concatenate(parts, axis=1) * kg_ref[...]
        out_ref[0] = y.astype(BF16)


def _compress(cva, batch, seq, cmp_pos, cmp_w1, cmp_b1, cmp_w2, k_norm0):
    rows = seq // CMP_STRIDE
    x = cva.reshape(batch, rows, CMP_STRIDE * 2 * KV_A)
    pos = jnp.tile(cmp_pos, (1, 1, N_KV_A))
    w1 = cmp_w1.reshape(2, CMP_LEN, HEAD_DIM, CMP_HIDDEN).astype(BF16)
    z1 = jnp.zeros_like(w1)
    w1 = jnp.concatenate([jnp.concatenate([w1, z1], axis=3), jnp.concatenate([z1, w1], axis=3)], axis=2)
    b1 = jnp.tile(cmp_b1, (1, N_KV_A)).reshape(2, 1, 2 * CMP_HIDDEN)
    w2 = cmp_w2.astype(BF16)
    z2 = jnp.zeros_like(w2)
    w2 = jnp.concatenate([jnp.concatenate([w2, z2], axis=2), jnp.concatenate([z2, w2], axis=2)], axis=1)
    kg = jnp.tile(k_norm0, N_KV_A).reshape(1, KV_A)
    full = lambda a: pl.BlockSpec(a.shape, lambda b: (0,) * a.ndim)
    out = pl.BlockSpec((1, rows, KV_A), lambda b: (b, 0, 0))
    return pl.pallas_call(
        _compress_kernel,
        grid=(batch,),
        in_specs=[pl.BlockSpec((1, rows, x.shape[2]), lambda b: (b, 0, 0)),
                  full(pos), full(w1), full(b1), full(w2), full(kg)],
        out_specs=[out, out],
        out_shape=[jax.ShapeDtypeStruct((batch, rows, KV_A), BF16)] * 2,
        compiler_params=_cparams(("parallel",)),
        name="nsa_compress",
    )(x, pos, w1, b1, w2, kg)


def _toeplitz_bias(tbl_ref, g, base, n_a, n_c):
    far = tbl_ref.shape[1] - 2
    rows = []
    for a in range(n_a):
        tiles = [tbl_ref[g * HPG_A:(g + 1) * HPG_A, jnp.clip(base + a - c, -1, far) + 1] for c in range(n_c)]
        rows.append(jnp.concatenate(tiles, axis=2))
    return jnp.concatenate(rows, axis=1)


def _nsa_kernel(q_ref, kc_ref, vc_ref, ksw_ref, ga_ref, bc_ref, bs_ref, bw_ref, ex_ref, gp_ref, o_ref,
                imp_ref, *, seq):
    tq, tk = TQ_NSA, TK_NSA
    qi = pl.program_id(1)
    rows_c = seq // CMP_STRIDE
    n_sb = seq // SEL_BLOCK
    k_sel = min(SEL_TOPK, n_sb)
    rows = HPG_A * tq
    groups = range(N_KV_A)

    lane = lax.broadcasted_iota(jnp.int32, (1, LANES), 1)
    low_half = lane < HEAD_DIM
    half_bf = [jnp.where(low_half, 1.0, 0.0).astype(BF16), jnp.where(low_half, 0.0, 1.0).astype(BF16)]
    qs = [jnp.concatenate([q_ref[:, c * LANES:(c + 1) * LANES] * half_bf[g] for c in range(HPG_A)], axis=0)
          for g in groups]

    def row_sums(p_bf):
        return _dot(p_bf, jnp.ones((p_bf.shape[1], LANES), BF16))

    jj = lax.broadcasted_iota(jnp.int32, (n_sb, rows_c), 0) * SEL_BLOCK
    nn = lax.broadcasted_iota(jnp.int32, (n_sb, rows_c), 1) * CMP_STRIDE
    ov = jnp.maximum(jnp.minimum(nn + CMP_LEN, jj + SEL_BLOCK) - jnp.maximum(nn, jj), 0)
    ov_t = (ov.astype(F32) * (1.0 / CMP_LEN)).astype(BF16)
    blk = lax.broadcasted_iota(jnp.int32, (n_sb, tq), 0)
    tpos = qi * tq + lax.broadcasted_iota(jnp.int32, (n_sb, tq), 1)
    cur = tpos // SEL_BLOCK
    forced = (blk == 0) | (blk == cur) | (blk == cur - 1)
    future = blk * SEL_BLOCK > tpos

    o_c, sel_bf = [], []
    for g in groups:
        s = _dot_nt(qs[g], kc_ref[0]).reshape(HPG_A, tq, rows_c) + bc_ref[g * HPG_A:(g + 1) * HPG_A, 0]
        m = jnp.max(s, axis=-1, keepdims=True)
        e = jnp.where(s > 0.5 * NEG, jnp.exp(s - m), 0.0)
        e_bf = e.reshape(rows, rows_c).astype(BF16)
        inv = 1.0 / jnp.maximum(row_sums(e_bf), 1e-30)
        o_c.append(_dot(e_bf, vc_ref[0]) * inv)
        inv4 = inv.reshape(HPG_A, tq, LANES)
        p = e * jnp.concatenate([inv4] * (rows_c // LANES), axis=2)
        p_sum = p[0] + p[1] + p[2] + p[3]
        p_hi, p_lo = _split_bf16(p_sum)
        p_lo2 = (p_sum - p_hi.astype(F32) - p_lo.astype(F32)).astype(BF16)
        imp = _dot_nt(ov_t, p_hi) + _dot_nt(ov_t, p_lo) + _dot_nt(ov_t, p_lo2)
        imp = jnp.where(forced, 1e6, jnp.where(future, -1e6, imp))
        imp_ref[...] = imp
        cnt = jnp.zeros((n_sb, tq), F32)
        for i in range(n_sb):
            ri = jnp.broadcast_to(imp_ref[pl.ds(i, 1), :], (n_sb, tq))
            later = jnp.where(blk > i, 1.0, 0.0)
            cnt = cnt + jnp.where(ri > imp, 1.0, jnp.where(ri == imp, later, 0.0))
        sel_bf.append(jnp.where(cnt < k_sel, 1.0, 0.0).T.astype(BF16))

    def with_ones(v_pair, g):
        return v_pair * half_bf[g] + half_bf[1 - g]

    def normalised(acc):
        return acc / pltpu.roll(acc, HEAD_DIM, axis=1)

    def sel_body(kj, carry):
        r0 = pl.multiple_of(kj * tk, tk)
        k = ksw_ref[pl.ds(r0, tk), 0:KV_A]
        v = ksw_ref[pl.ds(r0, tk), 2 * KV_A:3 * KV_A]
        out = []
        for g in groups:
            m, acc = carry[g]
            s = _dot_nt(qs[g], k).reshape(HPG_A, tq, tk)
            s = s + _toeplitz_bias(bs_ref, g, (tq // TOEP) * qi - (tk // TOEP) * kj, tq // TOEP, tk // TOEP)
            madd = (_dot(sel_bf[g], ex_ref[kj]) - 1.0) * (-NEG)
            s = (s + madd[None]).reshape(rows, tk)
            m_new = jnp.maximum(m, jnp.max(s, axis=-1, keepdims=True))
            p = jnp.exp(s - m_new).astype(BF16)
            out.append((m_new, jnp.exp(m - m_new) * acc + _dot(p, with_ones(v, g))))
        return tuple(out)

    init = tuple((jnp.full((rows, 1), NEG, F32), jnp.zeros((rows, LANES), F32)) for _ in groups)
    n_tiles = (qi * tq + tq + tk - 1) // tk
    sel_out = lax.fori_loop(0, n_tiles, sel_body, init)
    o_s = [normalised(acc) for _, acc in sel_out]

    n_wk = WIN_A + tq
    start = pl.multiple_of(jnp.maximum(qi * tq - WIN_A, 0), tq)
    kw = ksw_ref[pl.ds(start, n_wk), KV_A:2 * KV_A]
    vw = ksw_ref[pl.ds(start, n_wk), 3 * KV_A:4 * KV_A]
    o_w = []
    for g in groups:
        s = _dot_nt(qs[g], kw).reshape(HPG_A, tq, n_wk)
        s = s + _toeplitz_bias(bw_ref, g, (qi * tq - start) // TOEP, tq // TOEP, n_wk // TOEP)
        s = s.reshape(rows, n_wk)
        p = jnp.exp(s - jnp.max(s, axis=-1, keepdims=True)).astype(BF16)
        o_w.append(normalised(_dot(p, with_ones(vw, g))))

    g_hi, g_lo = _split_bf16(ga_ref[...])
    gates = _dot(g_hi, gp_ref[...]) + _dot(g_lo, gp_ref[...])
    for c in range(HPG_A):
        rs = slice(c * tq, (c + 1) * tq)
        out = jnp.zeros((tq, LANES), F32)
        for j, o in enumerate((o_c, o_s, o_w)):
            gate = gates[:, j * WIDTH_A + c * LANES:j * WIDTH_A + (c + 1) * LANES]
            out = out + gate * jnp.where(low_half, o[0][rs], o[1][rs])
        o_ref[:, c * LANES:(c + 1) * LANES] = out


def _nsa_attention(qa, kc, vc, ksw, ga, bias_c, bias_s, bias_w, batch, seq):
    tq, tk = TQ_NSA, TK_NSA
    nq = seq // tq
    n = batch * seq
    rows_c = seq // CMP_STRIDE
    n_sb = seq // SEL_BLOCK
    assert rows_c % LANES == 0 and seq % tk == 0 and seq >= WIN_A + tq and WIN_A % tq == 0
    key_blk = (np.arange(seq) // SEL_BLOCK).reshape(seq // tk, 1, tk)
    expand = jnp.asarray((key_blk == np.arange(n_sb).reshape(1, n_sb, 1)).astype(np.float32), BF16)
    gp = np.zeros((LANES, 3 * WIDTH_A), np.float32)
    for pos, head in enumerate(PAIRED_HEADS_A):
        for j in range(3):
            gp[3 * head + j, j * WIDTH_A + pos * HEAD_DIM:j * WIDTH_A + (pos + 1) * HEAD_DIM] = 1.0
    gp = jnp.asarray(gp, BF16)
    resident = lambda a: pl.BlockSpec(a.shape, lambda b, i: (0,) * a.ndim, pipeline_mode=pl.Buffered(1))
    return pl.pallas_call(
        functools.partial(_nsa_kernel, seq=seq),
        grid=(batch, nq),
        in_specs=[
            pl.BlockSpec((tq, WIDTH_A), lambda b, i: (b * nq + i, 0)),
            pl.BlockSpec((1, rows_c, KV_A), lambda b, i: (b, 0, 0)),
            pl.BlockSpec((1, rows_c, KV_A), lambda b, i: (b, 0, 0)),
            pl.BlockSpec((seq, 4 * KV_A), lambda b, i: (b, 0)),
            pl.BlockSpec((tq, LANES), lambda b, i: (b * nq + i, 0)),
            pl.BlockSpec((N_HEADS_A, 1, tq, rows_c), lambda b, i: (0, i, 0, 0)),
            resident(bias_s), resident(bias_w), resident(expand), resident(gp),
        ],
        out_specs=pl.BlockSpec((tq, WIDTH_A), lambda b, i: (b * nq + i, 0)),
        out_shape=jax.ShapeDtypeStruct((n, WIDTH_A), F32),
        scratch_shapes=[pltpu.VMEM((n_sb, tq), F32)],
        compiler_params=_cparams(("parallel", "arbitrary")),
        name="nsa_attention",
    )(qa, kc, vc, ksw, ga, bias_c, bias_s, bias_w, expand, gp)


def _dilated_kernel(q_ref, kp_ref, kc_ref, vp_ref, vc_ref, bias_ref, o_ref, lse_ref):
    band = DIL_BAND
    first = pl.program_id(2) == 0
    prev_mask = jnp.where(first, NEG, 0.0)
    lane = lax.broadcasted_iota(jnp.int32, (1, LANES), 1)
    low_half = lane < HEAD_DIM
    half_bf = [jnp.where(low_half, 1.0, 0.0).astype(BF16), jnp.where(low_half, 0.0, 1.0).astype(BF16)]
    den_sel = [jnp.broadcast_to(hm, (2 * band, LANES)) for hm in half_bf]
    key_col = lax.broadcasted_iota(jnp.int32, (1, 2 * band), 1)
    prev_cols = jnp.where(key_col < band, prev_mask, 0.0)
    for sub in range(q_ref.shape[1] // band):
        r_cur = slice(sub * band, (sub + 1) * band)
        for c in range(WIDTH_B // LANES):
            cols = slice(c * LANES, (c + 1) * LANES)
            q = q_ref[0, r_cur, cols]
            if sub == 0:
                k_cat = jnp.concatenate([kp_ref[0, :, cols], kc_ref[0, r_cur, cols]], axis=0)
                v_cat = jnp.concatenate([vp_ref[0, :, cols], vc_ref[0, r_cur, cols]], axis=0)
            else:
                k_cat = kc_ref[0, (sub - 1) * band:(sub + 1) * band, cols]
                v_cat = vc_ref[0, (sub - 1) * band:(sub + 1) * band, cols]
            acc = jnp.zeros((band, 2 * LANES), F32)
            ms = []
            for j in range(2):
                h = 2 * c + j
                s = _dot_nt(q * half_bf[j], k_cat) + bias_ref[h, 0]
                if sub == 0:
                    s = s + prev_cols
                m = jnp.max(s, axis=-1, keepdims=True)
                e = jnp.exp(s - m).astype(BF16)
                acc = acc + _dot(e, jnp.concatenate([v_cat * half_bf[j], den_sel[j]], axis=1))
                ms.append(m)
            den = acc[:, LANES:]
            o_ref[0, r_cur, cols] = acc[:, :LANES] / den
            lse_ref[0, r_cur, cols] = jnp.where(low_half, ms[0], ms[1]) + jnp.log(den)


def _dilated_attention(qb, kb, vb, bias, batch, seq, dil):
    band, rows = DIL_BAND, DIL_ROWS
    length = seq // dil
    sub = rows // band
    view = lambda a: a.reshape(batch, length, dil * WIDTH_B)
    cur = pl.BlockSpec((1, rows, WIDTH_B), lambda b, r, n: (b, n, r))
    prev = pl.BlockSpec((1, band, WIDTH_B), lambda b, r, n: (b, jnp.maximum(n * sub - 1, 0), r))
    o, lse = pl.pallas_call(
        _dilated_kernel,
        grid=(batch, dil, length // rows),
        in_specs=[cur, prev, cur, prev, cur,
                  pl.BlockSpec(bias.shape, lambda b, r, n: (0, 0, 0, 0))],
        out_specs=[cur, cur],
        out_shape=[jax.ShapeDtypeStruct((batch, length, dil * WIDTH_B), F32)] * 2,
        compiler_params=_cparams(("parallel", "parallel", "arbitrary")),
        name=f"dilated_attention_d{dil}",
    )(view(qb), view(kb), view(kb), view(vb), view(vb), bias)
    n = batch * seq
    return o.reshape(n, WIDTH_B), lse.reshape(n, WIDTH_B)


def _outproj_kernel(*refs, with_router):
    if with_router:
        (oa_ref, o1_ref, o2_ref, o3_ref, l1_ref, l2_ref, l3_ref, h_ref, gn_ref, w_ref, fg_ref, rw_ref,
         h_out_ref, v_ref, route_ref) = refs
    else:
        (oa_ref, o1_ref, o2_ref, o3_ref, l1_ref, l2_ref, l3_ref, h_ref, gn_ref, w_ref, fg_ref,
         h_out_ref, v_ref) = refs
    l1, l2, l3 = l1_ref[...], l2_ref[...], l3_ref[...]
    m = jnp.maximum(jnp.maximum(l1, l2), l3)
    e1, e2, e3 = jnp.exp(l1 - m), jnp.exp(l2 - m), jnp.exp(l3 - m)
    ob = (e1 * o1_ref[...] + e2 * o2_ref[...] + e3 * o3_ref[...]) / (e1 + e2 + e3)

    def norm(x, g):
        return x * lax.rsqrt(jnp.mean(x * x, axis=-1, keepdims=True) + RMS_EPS) * g

    o = jnp.concatenate([norm(oa_ref[...], gn_ref[:, :WIDTH_A]), norm(ob, gn_ref[:, WIDTH_A:])], axis=1)
    h = h_ref[...] + _dot(o.astype(BF16), w_ref[...])
    h_out_ref[...] = h
    v = norm(h, fg_ref[...])
    v_ref[...] = v.astype(v_ref.dtype)
    if with_router:
        v_hi, v_lo = _split_bf16(v)
        w_hi, w_lo = _split_bf16(rw_ref[...])
        logits = _dot(v_hi, w_hi) + _dot(v_lo, w_hi) + _dot(v_hi, w_lo)
        lane = lax.broadcasted_iota(jnp.int32, logits.shape, 1).astype(F32)
        lg = jnp.where(lane < N_EXPERTS, logits, -jnp.inf)
        m1 = jnp.max(lg, axis=-1, keepdims=True)
        i1 = jnp.min(jnp.where(lg == m1, lane, float(LANES)), axis=-1, keepdims=True)
        lg2 = jnp.where(lane == i1, -jnp.inf, lg)
        m2 = jnp.max(lg2, axis=-1, keepdims=True)
        i2 = jnp.min(jnp.where(lg2 == m2, lane, float(LANES)), axis=-1, keepdims=True)
        e = jnp.exp(m2 - m1)
        g1 = 1.0 / (1.0 + e)
        g2 = e / (1.0 + e)
        route_ref[...] = jnp.where(lane == 0, i1, jnp.where(
            lane == 1, i2, jnp.where(lane == 2, g1, jnp.where(lane == 3, g2, 0.0))))


def _out_projection(o_a, dil_outs, h, out_norm, w_out, ffn_norm, router_w):
    n = h.shape[0]
    tm = TM_PROJ
    with_router = router_w is not None
    row = lambda width: pl.BlockSpec((tm, width), lambda i: (i, 0))
    full = lambda a: pl.BlockSpec(a.shape, lambda i: (0,) * a.ndim)
    (o1, l1), (o2, l2), (o3, l3) = dil_outs
    gn_a = out_norm[:WIDTH_A].reshape(N_HEADS_A, HEAD_DIM)[PAIRED_HEADS_A].reshape(WIDTH_A)
    gn = jnp.concatenate([gn_a, out_norm[WIDTH_A:]]).reshape(1, -1)
    fg = ffn_norm.reshape(1, -1)
    w_a = w_out[:WIDTH_A].reshape(N_HEADS_A, HEAD_DIM, D_MODEL)[PAIRED_HEADS_A].reshape(WIDTH_A, D_MODEL)
    w = jnp.concatenate([w_a, w_out[WIDTH_A:]], axis=0).astype(BF16)
    args = [o_a, o1, o2, o3, l1, l2, l3, h, gn, w, fg]
    in_specs = [row(WIDTH_A)] + [row(WIDTH_B)] * 6 + [row(D_MODEL), full(gn), full(w), full(fg)]
    out_specs = [row(D_MODEL), row(D_MODEL)]
    out_shape = [jax.ShapeDtypeStruct((n, D_MODEL), F32),
                 jax.ShapeDtypeStruct((n, D_MODEL), F32 if with_router else BF16)]
    if with_router:
        rw = jnp.pad(router_w, ((0, 0), (0, LANES - N_EXPERTS)))
        args.append(rw)
        in_specs.append(full(rw))
        out_specs.append(row(LANES))
        out_shape.append(jax.ShapeDtypeStruct((n, LANES), F32))
    return pl.pallas_call(
        functools.partial(_outproj_kernel, with_router=with_router),
        grid=(n // tm,),
        in_specs=in_specs,
        out_specs=out_specs,
        out_shape=out_shape,
        compiler_params=_cparams(("parallel",)),
        name="out_projection",
    )(*args)


def _ffn_kernel(v_ref, h_ref, wg_ref, wu_ref, wd_ref, o_ref, acc_ref):
    f = pl.program_id(1)

    @pl.when(f == 0)
    def _():
        acc_ref[...] = jnp.zeros_like(acc_ref)

    v = v_ref[...]
    gate = _dot(v, wg_ref[...])
    up = _dot(v, wu_ref[...])
    acc_ref[...] += _dot((jax.nn.silu(gate) * up).astype(BF16), wd_ref[...])

    @pl.when(f == pl.num_programs(1) - 1)
    def _():
        o_ref[...] = h_ref[...] + acc_ref[...]


def _dense_ffn(v, h, w_gate, w_up, w_down):
    n = h.shape[0]
    d_ff = w_gate.shape[1]
    tm, tf = TM_FFN, TF_FFN
    return pl.pallas_call(
        _ffn_kernel,
        grid=(n // tm, d_ff // tf),
        in_specs=[pl.BlockSpec((tm, D_MODEL), lambda i, f: (i, 0)),
                  pl.BlockSpec((tm, D_MODEL), lambda i, f: (i, 0)),
                  pl.BlockSpec((D_MODEL, tf), lambda i, f: (0, f)),
                  pl.BlockSpec((D_MODEL, tf), lambda i, f: (0, f)),
                  pl.BlockSpec((tf, D_MODEL), lambda i, f: (f, 0))],
        out_specs=pl.BlockSpec((tm, D_MODEL), lambda i, f: (i, 0)),
        out_shape=jax.ShapeDtypeStruct((n, D_MODEL), F32),
        scratch_shapes=[pltpu.VMEM((tm, D_MODEL), F32)],
        compiler_params=_cparams(("parallel", "arbitrary")),
        name="dense_ffn",
    )(v, h, w_gate.astype(BF16), w_up.astype(BF16), w_down.astype(BF16))


def _row_copy(src_ref, src_row, dst_ref, dst_row, sem):
    return pltpu.make_async_copy(src_ref.at[pl.ds(src_row, 1)], dst_ref.at[pl.ds(dst_row, 1)], sem)


def _dispatch_kernel(dest_ref, v_ref, xs_in_ref, xs_ref, sem):
    del xs_in_ref
    n_slots = dest_ref.shape[2]

    def issue(i, carry):
        _row_copy(v_ref, i // TOP_K, xs_ref, dest_ref[0, 0, i], sem).start()
        return carry

    lax.fori_loop(0, n_slots, issue, 0)
    pltpu.make_async_copy(xs_ref.at[pl.ds(0, n_slots)], xs_ref.at[pl.ds(0, n_slots)], sem).wait()


def _dispatch(v, dest, n_rows):
    n = v.shape[0]
    tokens = DISPATCH_TOKENS
    slots = tokens * TOP_K
    return pl.pallas_call(
        _dispatch_kernel,
        grid=(n // tokens,),
        in_specs=[pl.BlockSpec((1, 1, slots), lambda i: (i, 0, 0), memory_space=pltpu.SMEM),
                  pl.BlockSpec((tokens, D_MODEL), lambda i: (i, 0)),
                  pl.BlockSpec(memory_space=pl.ANY)],
        out_specs=pl.BlockSpec(memory_space=pl.ANY),
        out_shape=jax.ShapeDtypeStruct((n_rows, D_MODEL), F32),
        scratch_shapes=[pltpu.SemaphoreType.DMA(())],
        input_output_aliases={2: 0},
        compiler_params=_cparams(("arbitrary",)),
        name="moe_dispatch",
    )(dest.reshape(n // tokens, 1, slots), v, jnp.zeros((n_rows, D_MODEL), F32))


def _gmm_kernel(te_ref, tv_ref, x_ref, wg_ref, wu_ref, wd_ref, y_ref, xb_ref, acc_ref):
    c = pl.program_id(0)
    f = pl.program_id(1)
    valid = tv_ref[c] > 0

    @pl.when(f == 0)
    def _():
        acc_ref[...] = jnp.zeros_like(acc_ref)
        xb_ref[...] = x_ref[...].astype(BF16)

    @pl.when(valid)
    def _():
        x = xb_ref[...]
        gate = _dot(x, wg_ref[0])
        up = _dot(x, wu_ref[0])
        acc_ref[...] += _dot((jax.nn.silu(gate) * up).astype(BF16), wd_ref[0])

    @pl.when(f == pl.num_programs(1) - 1)
    def _():
        y_ref[...] = acc_ref[...]


def _grouped_swiglu(xs, tile_expert, tile_valid, w_gate, w_up, w_down):
    n_rows = xs.shape[0]
    d_ff = w_gate.shape[2]
    tm, tf = TM_MOE, TF_MOE
    grid_spec = pltpu.PrefetchScalarGridSpec(
        num_scalar_prefetch=2,
        grid=(n_rows // tm, d_ff // tf),
        in_specs=[pl.BlockSpec((tm, D_MODEL), lambda c, f, te, tv: (c, 0)),
                  pl.BlockSpec((1, D_MODEL, tf), lambda c, f, te, tv: (te[c], 0, f)),
                  pl.BlockSpec((1, D_MODEL, tf), lambda c, f, te, tv: (te[c], 0, f)),
                  pl.BlockSpec((1, tf, D_MODEL), lambda c, f, te, tv: (te[c], f, 0))],
        out_specs=pl.BlockSpec((tm, D_MODEL), lambda c, f, te, tv: (c, 0)),
        scratch_shapes=[pltpu.VMEM((tm, D_MODEL), BF16), pltpu.VMEM((tm, D_MODEL), F32)],
    )
    return pl.pallas_call(
        _gmm_kernel,
        grid_spec=grid_spec,
        out_shape=jax.ShapeDtypeStruct((n_rows, D_MODEL), F32),
        compiler_params=_cparams(("parallel", "arbitrary")),
        name="moe_grouped_swiglu",
    )(tile_expert, tile_valid, xs, w_gate.astype(BF16), w_up.astype(BF16), w_down.astype(BF16))


def _combine_kernel(dest_ref, y_ref, route_ref, h_ref, o_ref, buf_ref, sem):
    tc = h_ref.shape[0]

    def issue(i, carry):
        for k in range(TOP_K):
            _row_copy(y_ref, dest_ref[0, 0, TOP_K * i + k], buf_ref.at[k], i, sem).start()
        return carry

    lax.fori_loop(0, tc, issue, 0)
    for k in range(TOP_K):
        pltpu.make_async_copy(y_ref.at[pl.ds(0, tc)], buf_ref.at[k], sem).wait()
    g = route_ref[...]
    o_ref[...] = h_ref[...] + g[:, 2:3] * buf_ref[0] + g[:, 3:4] * buf_ref[1]


def _combine(y, dest, route, h):
    n = h.shape[0]
    tc = COMBINE_TOKENS
    return pl.pallas_call(
        _combine_kernel,
        grid=(n // tc,),
        in_specs=[pl.BlockSpec((1, 1, tc * TOP_K), lambda i: (i, 0, 0), memory_space=pltpu.SMEM),
                  pl.BlockSpec(memory_space=pl.ANY),
                  pl.BlockSpec((tc, LANES), lambda i: (i, 0)),
                  pl.BlockSpec((tc, D_MODEL), lambda i: (i, 0))],
        out_specs=pl.BlockSpec((tc, D_MODEL), lambda i: (i, 0)),
        out_shape=jax.ShapeDtypeStruct((n, D_MODEL), F32),
        scratch_shapes=[pltpu.VMEM((TOP_K, tc, D_MODEL), F32), pltpu.SemaphoreType.DMA(())],
        compiler_params=_cparams(("arbitrary",)),
        name="moe_combine",
    )(dest.reshape(n // tc, 1, tc * TOP_K), y, route, h)


def _moe(v, route, h, w_gate, w_up, w_down):
    n = h.shape[0]
    nk = n * TOP_K
    tm = TM_MOE
    e_flat = route[:, :TOP_K].astype(jnp.int32).reshape(nk)
    onehot = (e_flat[:, None] == jnp.arange(N_EXPERTS)[None, :]).astype(jnp.int32)
    csum = jnp.cumsum(onehot, axis=0)
    counts = csum[-1]
    rank = jnp.sum(jnp.where(onehot > 0, csum, 0), axis=1) - 1
    padded = (counts + tm - 1) // tm * tm
    pend = jnp.cumsum(padded)
    pstart = pend - padded
    dest = (jnp.sum(jnp.where(onehot > 0, pstart[None, :], 0), axis=1) + rank).astype(jnp.int32)
    n_tiles = nk // tm + N_EXPERTS
    tile_start = jnp.arange(n_tiles) * tm
    tile_expert = jnp.minimum(jnp.sum(tile_start[:, None] >= pend[None, :], axis=1), N_EXPERTS - 1)
    tile_valid = (tile_start < pend[-1]).astype(jnp.int32)
    xs = _dispatch(v, dest, n_tiles * tm)
    y = _grouped_swiglu(xs, tile_expert.astype(jnp.int32), tile_valid, w_gate, w_up, w_down)
    return _combine(y, dest, route, h)


def kernel(x, rel_bias, attn_norm, w_in, nsa_q_norm, nsa_k_norm, cmp_pos, cmp_w1, cmp_b1, cmp_w2,
           dil_q_norm, dil_k_norm, out_norm, w_out, ffn_norm, ffn_w_gate, ffn_w_up, ffn_w_down,
           router_w, exp_w_gate, exp_w_up, exp_w_down):
    batch, seq, _ = x.shape
    depth = attn_norm.shape[0]
    n = batch * seq
    assert seq % (DIL_PAIRS[-1][1] * DIL_ROWS) == 0 and n % DISPATCH_TOKENS == 0
    assert all(w == d * DIL_BAND for w, d in DIL_PAIRS)

    tbl = rel_bias.astype(F32).T
    tbl_a, tbl_b = tbl[:N_HEADS_A], tbl[N_HEADS_A:]
    max_off = seq // TOEP - 1
    far_sel = min(max_off, -(-(FAR_DIST + TOEP - 1) // TOEP))
    far_win = min(max_off, -(-(WIN_A + TOEP - 1) // TOEP))
    bias_c = _bias_table(tbl_a, seq // TQ_NSA, TQ_NSA, seq // CMP_STRIDE,
                         functools.partial(_cmp_dist, n_cmp=(seq - CMP_LEN) // CMP_STRIDE + 1))
    bias_s = _bias_table(tbl_a, far_sel + 2, TOEP, TOEP, _sel_dist)
    bias_w = _bias_table(tbl_a, far_win + 2, TOEP, TOEP, _win_dist)
    bias_d = [_bias_table(tbl_b, 1, DIL_BAND, 2 * DIL_BAND, functools.partial(_dil_dist, dil=d))
              for _, d in DIL_PAIRS]

    h = x.reshape(n, D_MODEL)
    for layer in range(depth):
        qa, cva, ksw, qb, kb, vb, ga = _in_projection(
            h, attn_norm[layer], w_in[layer], nsa_q_norm[layer], nsa_k_norm[layer],
            dil_q_norm[layer], dil_k_norm[layer])
        kc, vc = _compress(cva, batch, seq, cmp_pos[layer], cmp_w1[layer], cmp_b1[layer], cmp_w2[layer],
                           nsa_k_norm[layer, 0])
        o_a = _nsa_attention(qa, kc, vc, ksw, ga, bias_c, bias_s, bias_w, batch, seq)
        dil_outs = [_dilated_attention(qb, kb, vb, bias_d[i], batch, seq, d)
                    for i, (_, d) in enumerate(DIL_PAIRS)]
        moe_layer = layer % 2 == 1
        outs = _out_projection(o_a, dil_outs, h, out_norm[layer], w_out[layer], ffn_norm[layer],
                               router_w[layer // 2] if moe_layer else None)
        if moe_layer:
            h1, v, route = outs
            h = _moe(v, route, h1, exp_w_gate[layer // 2], exp_w_up[layer // 2], exp_w_down[layer // 2])
        else:
            h1, v = outs
            h = _dense_ffn(v, h1, ffn_w_gate[layer // 2], ffn_w_up[layer // 2], ffn_w_down[layer // 2])
    return h.reshape(batch, seq, D_MODEL)
```

```python
import functools
import math

import jax
import jax.numpy as jnp
import numpy as np
from jax import lax
from jax.experimental import pallas as pl
from jax.experimental.pallas import tpu as pltpu

F32 = jnp.float32
BF16 = jnp.bfloat16

D_MODEL = 1024
HEAD_DIM = 64
N_HEADS_A = 8
N_KV_A = 2
HPG_A = N_HEADS_A // N_KV_A
N_HEADS_B = 8
WIDTH_A = N_HEADS_A * HEAD_DIM
WIDTH_B = N_HEADS_B * HEAD_DIM
KV_A = N_KV_A * HEAD_DIM
CMP_LEN = 32
CMP_STRIDE = 16
CMP_HIDDEN = 256
SEL_BLOCK = 64
SEL_TOPK = 16
WIN_A = 512
DIL_PAIRS = ((128, 1), (512, 4), (2048, 16))
N_BUCKETS = 32
MAX_DISTANCE = 2048
N_EXPERTS = 8
TOP_K = 2
RMS_EPS = 1e-6
NEG = -1e30
SCALE = HEAD_DIM ** -0.5

LANES = 128
MXU_DIM = 256
VMEM_LIMIT = 56 * 1024 * 1024

TM_PROJ = 512
TQ_NSA = 256
TK_NSA = 512
TOEP = 128
DIL_BAND = 128
DIL_ROWS = 256
TM_FFN = 512
TF_FFN = 1408
TM_MOE = 512
TF_MOE = 512
DISPATCH_TOKENS = 512
COMBINE_TOKENS = 256

COL_QA = 0
COL_CVA = 512
COL_KSW = 768
COL_QB = 1280
COL_KB = 1792
COL_VB = 2304
COL_GA = 2816
IN_COLS = 2944
PAIRED_HEADS_A = np.array([h + g * HPG_A for h in range(HPG_A) for g in range(N_KV_A)])


def _cparams(sem, vmem=VMEM_LIMIT):
    return pltpu.CompilerParams(dimension_semantics=sem, vmem_limit_bytes=vmem)


def _dot(a, b):
    return jnp.dot(a, b, preferred_element_type=F32)


def _dot_nt(a, b):
    return lax.dot_general(a, b, (((1,), (1,)), ((), ())), preferred_element_type=F32)


def _split_bf16(x):
    hi = x.astype(BF16)
    lo = (x - hi.astype(F32)).astype(BF16)
    return hi, lo


def _bucket_thresholds():
    d = np.arange(0, 4 * MAX_DISTANCE, dtype=np.int64)
    max_exact = N_BUCKETS // 2
    scaled = np.log(np.maximum(d, 1).astype(np.float32) / np.float32(max_exact)) / np.float32(
        math.log(MAX_DISTANCE / max_exact))
    large = np.minimum(max_exact + (scaled.astype(np.float32) * (N_BUCKETS - max_exact)).astype(np.int32),
                       N_BUCKETS - 1)
    bucket = np.where(d < max_exact, d, large)
    assert np.all(np.diff(bucket) >= 0)
    return [int(np.argmax(bucket >= b)) for b in range(N_BUCKETS)]


_THR = _bucket_thresholds()
FAR_DIST = _THR[N_BUCKETS - 1]


def _bias_table_kernel(tbl_ref, out_ref, *, n_heads, rows, cols, dist_valid):
    i = pl.program_id(0)
    a = lax.broadcasted_iota(jnp.int32, (rows, cols), 0)
    c = lax.broadcasted_iota(jnp.int32, (rows, cols), 1)
    d, valid = dist_valid(i, a, c)
    for h in range(n_heads):
        acc = jnp.full((rows, cols), tbl_ref[h, 0], F32)
        for b in range(1, N_BUCKETS):
            acc = jnp.where(d >= _THR[b], tbl_ref[h, b], acc)
        out_ref[h, 0] = jnp.where(valid, acc, NEG)


def _bias_table(tbl, n_tiles, rows, cols, dist_valid):
    n_heads = tbl.shape[0]
    return pl.pallas_call(
        functools.partial(_bias_table_kernel, n_heads=n_heads, rows=rows, cols=cols, dist_valid=dist_valid),
        grid=(n_tiles,),
        in_specs=[pl.BlockSpec(memory_space=pltpu.SMEM)],
        out_specs=pl.BlockSpec((n_heads, 1, rows, cols), lambda i: (0, i, 0, 0)),
        out_shape=jax.ShapeDtypeStruct((n_heads, n_tiles, rows, cols), F32),
        compiler_params=_cparams(("arbitrary",)),
        name="bias_table",
    )(tbl)


def _cmp_dist(i, a, c, *, n_cmp):
    d = i * TQ_NSA + a - (c * CMP_STRIDE + CMP_LEN - 1)
    return d, (d >= 0) & (c < n_cmp)


def _sel_dist(i, a, c):
    d = (i - 1) * TOEP + a - c
    return d, d >= 0


def _win_dist(i, a, c):
    d = (i - 1) * TOEP + a - c
    return d, (d >= 0) & (d < WIN_A)


def _dil_dist(i, a, c, *, dil):
    n = DIL_BAND + a - c
    return n * dil, (n >= 0) & (n <= DIL_BAND)


def _inproj_kernel(h_ref, gn_ref, w_ref, gain_ref, bd_ref,
                   qa_ref, cva_ref, ksw_ref, ga_ref, *rest):
    dil_refs, stage_ref = rest[:-1], rest[-1]
    tm = h_ref.shape[0]

    def emit_dilated(y, which):
        dil_refs[which][0, 0] = y.astype(BF16)
        n_chunks = y.shape[1] // LANES
        for c in range(n_chunks):
            stage_ref[c] = y[:, c * LANES:(c + 1) * LANES]
        for d, (_, r) in enumerate(DIL_PAIRS):
            if r == 1:
                continue
            ref = dil_refs[3 * d + which]
            for rho in range(r):
                for c in range(n_chunks):
                    ref[0, rho, :, c * LANES:(c + 1) * LANES] = stage_ref[
                        c, pl.ds(rho, tm // r, stride=r), :].astype(BF16)

    x = h_ref[...]
    u = (x * lax.rsqrt(jnp.mean(x * x, axis=-1, keepdims=True) + RMS_EPS) * gn_ref[...]).astype(BF16)

    def proj(c0, width):
        return _dot(u, w_ref[:, c0:c0 + width])

    def headnorm(acc, c0):
        outs = []
        for j in range(acc.shape[1] // MXU_DIM):
            a = acc[:, j * MXU_DIM:(j + 1) * MXU_DIM]
            sq_hi, sq_lo = _split_bf16(a * a)
            ms = _dot(sq_hi, bd_ref[...]) + _dot(sq_lo, bd_ref[...])
            g = gain_ref[:, c0 + j * MXU_DIM:c0 + (j + 1) * MXU_DIM]
            outs.append(a * lax.rsqrt(ms + RMS_EPS) * g)
        return outs[0] if len(outs) == 1 else jnp.concatenate(outs, axis=1)

    qa_ref[...] = headnorm(proj(COL_QA, WIDTH_A), COL_QA).astype(BF16)
    cva = proj(COL_CVA, 2 * KV_A)
    for c in range(2 * KV_A // LANES):
        stage_ref[c] = cva[:, c * LANES:(c + 1) * LANES]
    for l in range(CMP_STRIDE):
        for c in range(2 * KV_A // LANES):
            cva_ref[0, :, l * 2 * KV_A + c * LANES:l * 2 * KV_A + (c + 1) * LANES] = stage_ref[
                c, pl.ds(l, tm // CMP_STRIDE, stride=CMP_STRIDE), :]
    ksw = proj(COL_KSW, 4 * KV_A)
    ksw_ref[:, :2 * KV_A] = headnorm(ksw[:, :2 * KV_A], COL_KSW).astype(BF16)
    ksw_ref[:, 2 * KV_A:] = ksw[:, 2 * KV_A:].astype(BF16)
    emit_dilated(headnorm(proj(COL_QB, WIDTH_B), COL_QB), 0)
    emit_dilated(headnorm(proj(COL_KB, WIDTH_B), COL_KB), 1)
    emit_dilated(proj(COL_VB, WIDTH_B), 2)
    ga_ref[...] = jax.nn.sigmoid(proj(COL_GA, LANES))


def _in_projection(h, seq, attn_norm, w_in, nsa_q_norm, nsa_k_norm, dil_q_norm, dil_k_norm):
    n = h.shape[0]
    o = np.cumsum((0, WIDTH_A, KV_A, KV_A, KV_A, KV_A, KV_A, KV_A, N_HEADS_A * 3, WIDTH_B, WIDTH_B, WIDTH_B))
    seg = [w_in[:, o[i]:o[i + 1]] for i in range(11)]
    qa, kc, vc, ks, vs, kw, vw, ga, qb, kb, vb = seg
    qa = qa.reshape(D_MODEL, N_HEADS_A, HEAD_DIM)[:, PAIRED_HEADS_A, :].reshape(D_MODEL, WIDTH_A)
    pad = jnp.zeros((D_MODEL, IN_COLS - COL_GA - N_HEADS_A * 3), w_in.dtype)
    w = jnp.concatenate([qa, kc, vc, ks, kw, vs, vw, qb, kb, vb, ga, pad], axis=1).astype(BF16)
    ones = jnp.ones((IN_COLS,), F32)
    gain = ones
    gain = gain.at[COL_QA:COL_QA + WIDTH_A].set(jnp.tile(nsa_q_norm, N_HEADS_A) * SCALE)
    gain = gain.at[COL_KSW:COL_KSW + KV_A].set(jnp.tile(nsa_k_norm[1], N_KV_A))
    gain = gain.at[COL_KSW + KV_A:COL_KSW + 2 * KV_A].set(jnp.tile(nsa_k_norm[2], N_KV_A))
    gain = gain.at[COL_QB:COL_QB + WIDTH_B].set(jnp.tile(dil_q_norm, N_HEADS_B) * SCALE)
    gain = gain.at[COL_KB:COL_KB + WIDTH_B].set(jnp.tile(dil_k_norm, N_HEADS_B))
    blk = np.arange(MXU_DIM) // HEAD_DIM
    bd = jnp.asarray((blk[:, None] == blk[None, :]).astype(np.float32) / HEAD_DIM, BF16)

    tm = TM_PROJ
    tiles_per_seq = seq // tm
    row = lambda width: pl.BlockSpec((tm, width), lambda i: (i, 0))
    full = lambda a: pl.BlockSpec(a.shape, lambda i: (0,) * a.ndim)
    gn = attn_norm.reshape(1, D_MODEL)
    gain = gain.reshape(1, IN_COLS)
    dil_specs, dil_shapes = [], []
    for _, r in DIL_PAIRS:
        spec = pl.BlockSpec((1, r, tm // r, WIDTH_B), lambda i: (i // tiles_per_seq, 0, i % tiles_per_seq, 0))
        dil_specs += [spec] * 3
        dil_shapes += [jax.ShapeDtypeStruct((n // seq, r, seq // r, WIDTH_B), BF16)] * 3
    outs = pl.pallas_call(
        _inproj_kernel,
        grid=(n // tm,),
        in_specs=[row(D_MODEL), full(gn), full(w), full(gain), full(bd)],
        out_specs=[row(WIDTH_A),
                   pl.BlockSpec((1, tm // CMP_STRIDE, CMP_STRIDE * 2 * KV_A),
                                lambda i: (i // tiles_per_seq, i % tiles_per_seq, 0)),
                   row(4 * KV_A), row(LANES)] + dil_specs,
        out_shape=[jax.ShapeDtypeStruct((n, WIDTH_A), BF16),
                   jax.ShapeDtypeStruct((n // seq, seq // CMP_STRIDE, CMP_STRIDE * 2 * KV_A), F32),
                   jax.ShapeDtypeStruct((n, 4 * KV_A), BF16), jax.ShapeDtypeStruct((n, LANES), F32)] + dil_shapes,
        scratch_shapes=[pltpu.VMEM((WIDTH_B // LANES, tm, LANES), F32)],
        compiler_params=_cparams(("parallel",)),
        name="in_projection",
    )(h, gn, w, gain, bd)
    qa, cva, ksw, ga = outs[:4]
    qkv_dilated = [outs[4 + 3 * d:7 + 3 * d] for d in range(len(DIL_PAIRS))]
    return qa, cva, ksw, ga, qkv_dilated


def _gelu_tanh(x):
    return 0.5 * x * (1.0 + jnp.tanh(math.sqrt(2.0 / math.pi) * (x + 0.044715 * (x * x * x))))


def _compress_kernel(x_ref, pos_ref, w1_ref, b1_ref, w2_ref, kg_ref, kc_ref, vc_ref):
    rows = x_ref.shape[1]
    half = CMP_LEN // 2
    for which, out_ref in ((0, kc_ref), (1, vc_ref)):
        top = jnp.zeros((rows, 2 * CMP_HIDDEN), F32)
        bot = jnp.zeros((rows, 2 * CMP_HIDDEN), F32)
        for l in range(half):
            c0 = l * 2 * KV_A + which * KV_A
            a = x_ref[0, :, c0:c0 + KV_A]
            top += _dot((a + pos_ref[which, l:l + 1, :]).astype(BF16), w1_ref[which, l])
            bot += _dot((a + pos_ref[which, half + l:half + l + 1, :]).astype(BF16), w1_ref[which, half + l])
        hid = top + pltpu.roll(bot, rows - 1, axis=0) + b1_ref[which]
        y = _dot(_gelu_tanh(hid).astype(BF16), w2_ref[which])
        if which == 0:
            parts = []
            for g in range(N_KV_A):
                yg = y[:, g * HEAD_DIM:(g + 1) * HEAD_DIM]
                parts.append(yg * lax.rsqrt(jnp.mean(yg * yg, axis=-1, keepdims=True) + RMS_EPS))
            y = jnp.concatenate(parts, axis=1) * kg_ref[...]
        out_ref[0] = y.astype(BF16)


def _compress(x, cmp_pos, cmp_w1, cmp_b1, cmp_w2, k_norm0):
    batch, rows, _ = x.shape
    pos = jnp.tile(cmp_pos, (1, 1, N_KV_A))
    w1 = cmp_w1.reshape(2, CMP_LEN, HEAD_DIM, CMP_HIDDEN).astype(BF16)
    z1 = jnp.zeros_like(w1)
    w1 = jnp.concatenate([jnp.concatenate([w1, z1], axis=3), jnp.concatenate([z1, w1], axis=3)], axis=2)
    b1 = jnp.tile(cmp_b1, (1, N_KV_A)).reshape(2, 1, 2 * CMP_HIDDEN)
    w2 = cmp_w2.astype(BF16)
    z2 = jnp.zeros_like(w2)
    w2 = jnp.concatenate([jnp.concatenate([w2, z2], axis=2), jnp.concatenate([z2, w2], axis=2)], axis=1)
    kg = jnp.tile(k_norm0, N_KV_A).reshape(1, KV_A)
    full = lambda a: pl.BlockSpec(a.shape, lambda b: (0,) * a.ndim)
    out = pl.BlockSpec((1, rows, KV_A), lambda b: (b, 0, 0))
    return pl.pallas_call(
        _compress_kernel,
        grid=(batch,),
        in_specs=[pl.BlockSpec((1, rows, x.shape[2]), lambda b: (b, 0, 0)),
                  full(pos), full(w1), full(b1), full(w2), full(kg)],
        out_specs=[out, out],
        out_shape=[jax.ShapeDtypeStruct((batch, rows, KV_A), BF16)] * 2,
        compiler_params=_cparams(("parallel",)),
        name="nsa_compress",
    )(x, pos, w1, b1, w2, kg)


def _toeplitz_bias(tbl_ref, g, base, n_a, n_c):
    far = tbl_ref.shape[1] - 2
    rows = []
    for a in range(n_a):
        tiles = [tbl_ref[g * HPG_A:(g + 1) * HPG_A, jnp.clip(base + a - c, -1, far) + 1] for c in range(n_c)]
        rows.append(jnp.concatenate(tiles, axis=2))
    return jnp.concatenate(rows, axis=1)


def _nsa_kernel(q_ref, kc_ref, vc_ref, ksw_ref, ga_ref, bc_ref, bs_ref, bw_ref, ex_ref, gp_ref, o_ref,
                imp_ref, *, seq):
    tq, tk = TQ_NSA, TK_NSA
    qi = pl.program_id(1)
    rows_c = seq // CMP_STRIDE
    n_sb = seq // SEL_BLOCK
    k_sel = min(SEL_TOPK, n_sb)
    rows = HPG_A * tq
    groups = range(N_KV_A)

    lane = lax.broadcasted_iota(jnp.int32, (1, LANES), 1)
    low_half = lane < HEAD_DIM
    half_bf = [jnp.where(low_half, 1.0, 0.0).astype(BF16), jnp.where(low_half, 0.0, 1.0).astype(BF16)]
    qs = [jnp.concatenate([q_ref[:, c * LANES:(c + 1) * LANES] * half_bf[g] for c in range(HPG_A)], axis=0)
          for g in groups]

    def row_sums(p_bf):
        return _dot(p_bf, jnp.ones((p_bf.shape[1], LANES), BF16))

    jj = lax.broadcasted_iota(jnp.int32, (n_sb, rows_c), 0) * SEL_BLOCK
    nn = lax.broadcasted_iota(jnp.int32, (n_sb, rows_c), 1) * CMP_STRIDE
    ov = jnp.maximum(jnp.minimum(nn + CMP_LEN, jj + SEL_BLOCK) - jnp.maximum(nn, jj), 0)
    ov_t = (ov.astype(F32) * (1.0 / CMP_LEN)).astype(BF16)
    blk = lax.broadcasted_iota(jnp.int32, (n_sb, tq), 0)
    tpos = qi * tq + lax.broadcasted_iota(jnp.int32, (n_sb, tq), 1)
    cur = tpos // SEL_BLOCK
    forced = (blk == 0) | (blk == cur) | (blk == cur - 1)
    future = blk * SEL_BLOCK > tpos

    o_c, sel_bf = [], []
    for g in groups:
        s = _dot_nt(qs[g], kc_ref[0]).reshape(HPG_A, tq, rows_c) + bc_ref[g * HPG_A:(g + 1) * HPG_A, 0]
        m = jnp.max(s, axis=-1, keepdims=True)
        e = jnp.where(s > 0.5 * NEG, jnp.exp(s - m), 0.0)
        e_bf = e.reshape(rows, rows_c).astype(BF16)
        inv = 1.0 / jnp.maximum(row_sums(e_bf), 1e-30)
        o_c.append(_dot(e_bf, vc_ref[0]) * inv)
        inv4 = inv.reshape(HPG_A, tq, LANES)
        p = e * jnp.concatenate([inv4] * (rows_c // LANES), axis=2)
        p_sum = p[0] + p[1] + p[2] + p[3]
        p_hi, p_lo = _split_bf16(p_sum)
        p_lo2 = (p_sum - p_hi.astype(F32) - p_lo.astype(F32)).astype(BF16)
        imp = _dot_nt(ov_t, p_hi) + _dot_nt(ov_t, p_lo) + _dot_nt(ov_t, p_lo2)
        imp = jnp.where(forced, 1e6, jnp.where(future, -1e6, imp))
        imp_ref[...] = imp
        cnt = jnp.zeros((n_sb, tq), F32)
        for i in range(n_sb):
            ri = jnp.broadcast_to(imp_ref[pl.ds(i, 1), :], (n_sb, tq))
            later = jnp.where(blk > i, 1.0, 0.0)
            cnt = cnt + jnp.where(ri > imp, 1.0, jnp.where(ri == imp, later, 0.0))
        sel_bf.append(jnp.where(cnt < k_sel, 1.0, 0.0).T.astype(BF16))

    def with_ones(v_pair, g):
        return v_pair * half_bf[g] + half_bf[1 - g]

    def normalised(acc):
        return acc / pltpu.roll(acc, HEAD_DIM, axis=1)

    def sel_body(kj, carry):
        r0 = pl.multiple_of(kj * tk, tk)
        k = ksw_ref[pl.ds(r0, tk), 0:KV_A]
        v = ksw_ref[pl.ds(r0, tk), 2 * KV_A:3 * KV_A]
        out = []
        for g in groups:
            m, acc = carry[g]
            s = _dot_nt(qs[g], k).reshape(HPG_A, tq, tk)
            s = s + _toeplitz_bias(bs_ref, g, (tq // TOEP) * qi - (tk // TOEP) * kj, tq // TOEP, tk // TOEP)
            madd = (_dot(sel_bf[g], ex_ref[kj]) - 1.0) * (-NEG)
            s = (s + madd[None]).reshape(rows, tk)
            m_new = jnp.maximum(m, jnp.max(s, axis=-1, keepdims=True))
            p = jnp.exp(s - m_new).astype(BF16)
            out.append((m_new, jnp.exp(m - m_new) * acc + _dot(p, with_ones(v, g))))
        return tuple(out)

    init = tuple((jnp.full((rows, 1), NEG, F32), jnp.zeros((rows, LANES), F32)) for _ in groups)
    n_tiles = (qi * tq + tq + tk - 1) // tk
    sel_out = lax.fori_loop(0, n_tiles, sel_body, init)
    o_s = [normalised(acc) for _, acc in sel_out]

    n_wk = WIN_A + tq
    start = pl.multiple_of(jnp.maximum(qi * tq - WIN_A, 0), tq)
    kw = ksw_ref[pl.ds(start, n_wk), KV_A:2 * KV_A]
    vw = ksw_ref[pl.ds(start, n_wk), 3 * KV_A:4 * KV_A]
    o_w = []
    for g in groups:
        s = _dot_nt(qs[g], kw).reshape(HPG_A, tq, n_wk)
        s = s + _toeplitz_bias(bw_ref, g, (qi * tq - start) // TOEP, tq // TOEP, n_wk // TOEP)
        s = s.reshape(rows, n_wk)
        p = jnp.exp(s - jnp.max(s, axis=-1, keepdims=True)).astype(BF16)
        o_w.append(normalised(_dot(p, with_ones(vw, g))))

    g_hi, g_lo = _split_bf16(ga_ref[...])
    gates = _dot(g_hi, gp_ref[...]) + _dot(g_lo, gp_ref[...])
    for c in range(HPG_A):
        rs = slice(c * tq, (c + 1) * tq)
        out = jnp.zeros((tq, LANES), F32)
        for j, o in enumerate((o_c, o_s, o_w)):
            gate = gates[:, j * WIDTH_A + c * LANES:j * WIDTH_A + (c + 1) * LANES]
            out = out + gate * jnp.where(low_half, o[0][rs], o[1][rs])
        o_ref[:, c * LANES:(c + 1) * LANES] = out


def _nsa_attention(qa, kc, vc, ksw, ga, bias_c, bias_s, bias_w, batch, seq):
    tq, tk = TQ_NSA, TK_NSA
    nq = seq // tq
    n = batch * seq
    rows_c = seq // CMP_STRIDE
    n_sb = seq // SEL_BLOCK
    assert rows_c % LANES == 0 and seq % tk == 0 and seq >= WIN_A + tq and WIN_A % tq == 0
    key_blk = (np.arange(seq) // SEL_BLOCK).reshape(seq // tk, 1, tk)
    expand = jnp.asarray((key_blk == np.arange(n_sb).reshape(1, n_sb, 1)).astype(np.float32), BF16)
    gp = np.zeros((LANES, 3 * WIDTH_A), np.float32)
    for pos, head in enumerate(PAIRED_HEADS_A):
        for j in range(3):
            gp[3 * head + j, j * WIDTH_A + pos * HEAD_DIM:j * WIDTH_A + (pos + 1) * HEAD_DIM] = 1.0
    gp = jnp.asarray(gp, BF16)
    resident = lambda a: pl.BlockSpec(a.shape, lambda b, i: (0,) * a.ndim, pipeline_mode=pl.Buffered(1))
    return pl.pallas_call(
        functools.partial(_nsa_kernel, seq=seq),
        grid=(batch, nq),
        in_specs=[
            pl.BlockSpec((tq, WIDTH_A), lambda b, i: (b * nq + i, 0)),
            pl.BlockSpec((1, rows_c, KV_A), lambda b, i: (b, 0, 0)),
            pl.BlockSpec((1, rows_c, KV_A), lambda b, i: (b, 0, 0)),
            pl.BlockSpec((seq, 4 * KV_A), lambda b, i: (b, 0)),
            pl.BlockSpec((tq, LANES), lambda b, i: (b * nq + i, 0)),
            pl.BlockSpec((N_HEADS_A, 1, tq, rows_c), lambda b, i: (0, i, 0, 0)),
            resident(bias_s), resident(bias_w), resident(expand), resident(gp),
        ],
        out_specs=pl.BlockSpec((tq, WIDTH_A), lambda b, i: (b * nq + i, 0)),
        out_shape=jax.ShapeDtypeStruct((n, WIDTH_A), F32),
        scratch_shapes=[pltpu.VMEM((n_sb, tq), F32)],
        compiler_params=_cparams(("parallel", "arbitrary")),
        name="nsa_attention",
    )(qa, kc, vc, ksw, ga, bias_c, bias_s, bias_w, expand, gp)


def _dilated_kernel(q_ref, kp_ref, kc_ref, vp_ref, vc_ref, bias_ref, o_ref, lse_ref):
    band = DIL_BAND
    first = pl.program_id(2) == 0
    prev_mask = jnp.where(first, NEG, 0.0)
    lane = lax.broadcasted_iota(jnp.int32, (1, LANES), 1)
    low_half = lane < HEAD_DIM
    half_bf = [jnp.where(low_half, 1.0, 0.0).astype(BF16), jnp.where(low_half, 0.0, 1.0).astype(BF16)]
    den_sel = [jnp.broadcast_to(hm, (2 * band, LANES)) for hm in half_bf]
    key_col = lax.broadcasted_iota(jnp.int32, (1, 2 * band), 1)
    prev_cols = jnp.where(key_col < band, prev_mask, 0.0)
    for sub in range(q_ref.shape[0] // band):
        r_cur = slice(sub * band, (sub + 1) * band)
        for c in range(WIDTH_B // LANES):
            cols = slice(c * LANES, (c + 1) * LANES)
            q = q_ref[r_cur, cols]
            if sub == 0:
                k_cat = jnp.concatenate([kp_ref[:, cols], kc_ref[r_cur, cols]], axis=0)
                v_cat = jnp.concatenate([vp_ref[:, cols], vc_ref[r_cur, cols]], axis=0)
            else:
                k_cat = kc_ref[(sub - 1) * band:(sub + 1) * band, cols]
                v_cat = vc_ref[(sub - 1) * band:(sub + 1) * band, cols]
            acc = jnp.zeros((band, 2 * LANES), F32)
            ms = []
            for j in range(2):
                h = 2 * c + j
                s = _dot_nt(q * half_bf[j], k_cat) + bias_ref[h, 0]
                if sub == 0:
                    s = s + prev_cols
                m = jnp.max(s, axis=-1, keepdims=True)
                e = jnp.exp(s - m).astype(BF16)
                acc = acc + _dot(e, jnp.concatenate([v_cat * half_bf[j], den_sel[j]], axis=1))
                ms.append(m)
            den = acc[:, LANES:]
            o_ref[r_cur, cols] = acc[:, :LANES] / den
            lse_ref[r_cur, cols] = jnp.where(low_half, ms[0], ms[1]) + jnp.log(den)


def _dilated_attention(q, k, v, bias, dil):
    band, rows = DIL_BAND, DIL_ROWS
    batch, _, length, _ = q.shape
    sub = rows // band
    cur = pl.BlockSpec((None, None, rows, WIDTH_B), lambda b, r, n: (b, r, n, 0))
    prev = pl.BlockSpec((None, None, band, WIDTH_B), lambda b, r, n: (b, r, jnp.maximum(n * sub - 1, 0), 0))
    return pl.pallas_call(
        _dilated_kernel,
        grid=(batch, dil, length // rows),
        in_specs=[cur, prev, cur, prev, cur,
                  pl.BlockSpec(bias.shape, lambda b, r, n: (0, 0, 0, 0))],
        out_specs=[cur, cur],
        out_shape=[jax.ShapeDtypeStruct((batch, dil, length, WIDTH_B), F32)] * 2,
        compiler_params=_cparams(("parallel", "parallel", "arbitrary")),
        name=f"dilated_attention_d{dil}",
    )(q, k, k, v, v, bias)


def _outproj_kernel(*refs, with_router):
    if with_router:
        (oa_ref, o1_ref, o2_ref, o3_ref, l1_ref, l2_ref, l3_ref, h_ref, gn_ref, w_ref, fg_ref, rw_ref,
         h_out_ref, v_ref, route_ref, stage_ref) = refs
    else:
        (oa_ref, o1_ref, o2_ref, o3_ref, l1_ref, l2_ref, l3_ref, h_ref, gn_ref, w_ref, fg_ref,
         h_out_ref, v_ref, stage_ref) = refs
    tm = h_ref.shape[0]

    def token_order(ref, slot):
        r = ref.shape[1]
        if r == 1:
            return ref[0, 0]
        n_chunks = ref.shape[3] // LANES
        for rho in range(r):
            for c in range(n_chunks):
                stage_ref[slot, c, pl.ds(rho, tm // r, stride=r), :] = ref[0, rho, :, c * LANES:(c + 1) * LANES]
        return jnp.concatenate([stage_ref[slot, c] for c in range(n_chunks)], axis=1)

    l1, l2, l3 = token_order(l1_ref, 0), token_order(l2_ref, 0), token_order(l3_ref, 1)
    m = jnp.maximum(jnp.maximum(l1, l2), l3)
    e1, e2, e3 = jnp.exp(l1 - m), jnp.exp(l2 - m), jnp.exp(l3 - m)
    o1, o2, o3 = token_order(o1_ref, 0), token_order(o2_ref, 2), token_order(o3_ref, 3)
    ob = (e1 * o1 + e2 * o2 + e3 * o3) / (e1 + e2 + e3)

    def norm(x, g):
        return x * lax.rsqrt(jnp.mean(x * x, axis=-1, keepdims=True) + RMS_EPS) * g

    o = jnp.concatenate([norm(oa_ref[...], gn_ref[:, :WIDTH_A]), norm(ob, gn_ref[:, WIDTH_A:])], axis=1)
    h = h_ref[...] + _dot(o.astype(BF16), w_ref[...])
    h_out_ref[...] = h
    v = norm(h, fg_ref[...])
    v_ref[...] = v.astype(v_ref.dtype)
    if with_router:
        v_hi, v_lo = _split_bf16(v)
        w_hi, w_lo = _split_bf16(rw_ref[...])
        logits = _dot(v_hi, w_hi) + _dot(v_lo, w_hi) + _dot(v_hi, w_lo)
        lane = lax.broadcasted_iota(jnp.int32, logits.shape, 1).astype(F32)
        lg = jnp.where(lane < N_EXPERTS, logits, -jnp.inf)
        m1 = jnp.max(lg, axis=-1, keepdims=True)
        i1 = jnp.min(jnp.where(lg == m1, lane, float(LANES)), axis=-1, keepdims=True)
        lg2 = jnp.where(lane == i1, -jnp.inf, lg)
        m2 = jnp.max(lg2, axis=-1, keepdims=True)
        i2 = jnp.min(jnp.where(lg2 == m2, lane, float(LANES)), axis=-1, keepdims=True)
        e = jnp.exp(m2 - m1)
        g1 = 1.0 / (1.0 + e)
        g2 = e / (1.0 + e)
        route_ref[...] = jnp.where(lane == 0, i1, jnp.where(
            lane == 1, i2, jnp.where(lane == 2, g1, jnp.where(lane == 3, g2, 0.0))))


def _out_projection(o_a, dil_outs, h, out_norm, w_out, ffn_norm, router_w):
    n = h.shape[0]
    tm = TM_PROJ
    with_router = router_w is not None
    row = lambda width: pl.BlockSpec((tm, width), lambda i: (i, 0))
    full = lambda a: pl.BlockSpec(a.shape, lambda i: (0,) * a.ndim)
    (o1, l1), (o2, l2), (o3, l3) = dil_outs
    gn_a = out_norm[:WIDTH_A].reshape(N_HEADS_A, HEAD_DIM)[PAIRED_HEADS_A].reshape(WIDTH_A)
    gn = jnp.concatenate([gn_a, out_norm[WIDTH_A:]]).reshape(1, -1)
    fg = ffn_norm.reshape(1, -1)
    w_a = w_out[:WIDTH_A].reshape(N_HEADS_A, HEAD_DIM, D_MODEL)[PAIRED_HEADS_A].reshape(WIDTH_A, D_MODEL)
    w = jnp.concatenate([w_a, w_out[WIDTH_A:]], axis=0).astype(BF16)
    tiles_per_seq = o1.shape[2] // tm
    grouped = lambda a: pl.BlockSpec((1, a.shape[1], tm // a.shape[1], WIDTH_B),
                                     lambda i: (i // tiles_per_seq, 0, i % tiles_per_seq, 0))
    args = [o_a, o1, o2, o3, l1, l2, l3, h, gn, w, fg]
    in_specs = ([row(WIDTH_A)] + [grouped(a) for a in args[1:7]]
                + [row(D_MODEL), full(gn), full(w), full(fg)])
    out_specs = [row(D_MODEL), row(D_MODEL)]
    out_shape = [jax.ShapeDtypeStruct((n, D_MODEL), F32),
                 jax.ShapeDtypeStruct((n, D_MODEL), F32 if with_router else BF16)]
    if with_router:
        rw = jnp.pad(router_w, ((0, 0), (0, LANES - N_EXPERTS)))
        args.append(rw)
        in_specs.append(full(rw))
        out_specs.append(row(LANES))
        out_shape.append(jax.ShapeDtypeStruct((n, LANES), F32))
    return pl.pallas_call(
        functools.partial(_outproj_kernel, with_router=with_router),
        grid=(n // tm,),
        in_specs=in_specs,
        out_specs=out_specs,
        out_shape=out_shape,
        scratch_shapes=[pltpu.VMEM((4, WIDTH_B // LANES, tm, LANES), F32)],
        compiler_params=_cparams(("parallel",)),
        name="out_projection",
    )(*args)


def _ffn_kernel(v_ref, h_ref, wg_ref, wu_ref, wd_ref, o_ref, acc_ref):
    f = pl.program_id(1)

    @pl.when(f == 0)
    def _():
        acc_ref[...] = jnp.zeros_like(acc_ref)

    v = v_ref[...]
    gate = _dot(v, wg_ref[...])
    up = _dot(v, wu_ref[...])
    acc_ref[...] += _dot((jax.nn.silu(gate) * up).astype(BF16), wd_ref[...])

    @pl.when(f == pl.num_programs(1) - 1)
    def _():
        o_ref[...] = h_ref[...] + acc_ref[...]


def _dense_ffn(v, h, w_gate, w_up, w_down):
    n = h.shape[0]
    d_ff = w_gate.shape[1]
    tm, tf = TM_FFN, TF_FFN
    return pl.pallas_call(
        _ffn_kernel,
        grid=(n // tm, d_ff // tf),
        in_specs=[pl.BlockSpec((tm, D_MODEL), lambda i, f: (i, 0)),
                  pl.BlockSpec((tm, D_MODEL), lambda i, f: (i, 0)),
                  pl.BlockSpec((D_MODEL, tf), lambda i, f: (0, f)),
                  pl.BlockSpec((D_MODEL, tf), lambda i, f: (0, f)),
                  pl.BlockSpec((tf, D_MODEL), lambda i, f: (f, 0))],
        out_specs=pl.BlockSpec((tm, D_MODEL), lambda i, f: (i, 0)),
        out_shape=jax.ShapeDtypeStruct((n, D_MODEL), F32),
        scratch_shapes=[pltpu.VMEM((tm, D_MODEL), F32)],
        compiler_params=_cparams(("parallel", "arbitrary")),
        name="dense_ffn",
    )(v, h, w_gate.astype(BF16), w_up.astype(BF16), w_down.astype(BF16))


def _row_copy(src_ref, src_row, dst_ref, dst_row, sem):
    return pltpu.make_async_copy(src_ref.at[pl.ds(src_row, 1)], dst_ref.at[pl.ds(dst_row, 1)], sem)


def _dispatch_kernel(dest_ref, v_ref, xs_in_ref, xs_ref, sem):
    del xs_in_ref
    n_slots = dest_ref.shape[2]

    def issue(i, carry):
        _row_copy(v_ref, i // TOP_K, xs_ref, dest_ref[0, 0, i], sem).start()
        return carry

    lax.fori_loop(0, n_slots, issue, 0)
    pltpu.make_async_copy(xs_ref.at[pl.ds(0, n_slots)], xs_ref.at[pl.ds(0, n_slots)], sem).wait()


def _dispatch(v, dest, n_rows):
    n = v.shape[0]
    tokens = DISPATCH_TOKENS
    slots = tokens * TOP_K
    return pl.pallas_call(
        _dispatch_kernel,
        grid=(n // tokens,),
        in_specs=[pl.BlockSpec((1, 1, slots), lambda i: (i, 0, 0), memory_space=pltpu.SMEM),
                  pl.BlockSpec((tokens, D_MODEL), lambda i: (i, 0)),
                  pl.BlockSpec(memory_space=pl.ANY)],
        out_specs=pl.BlockSpec(memory_space=pl.ANY),
        out_shape=jax.ShapeDtypeStruct((n_rows, D_MODEL), F32),
        scratch_shapes=[pltpu.SemaphoreType.DMA(())],
        input_output_aliases={2: 0},
        compiler_params=_cparams(("arbitrary",)),
        name="moe_dispatch",
    )(dest.reshape(n // tokens, 1, slots), v, jnp.zeros((n_rows, D_MODEL), F32))


def _gmm_kernel(te_ref, tv_ref, x_ref, wg_ref, wu_ref, wd_ref, y_ref, xb_ref, acc_ref):
    c = pl.program_id(0)
    f = pl.program_id(1)
    valid = tv_ref[c] > 0

    @pl.when(f == 0)
    def _():
        acc_ref[...] = jnp.zeros_like(acc_ref)
        xb_ref[...] = x_ref[...].astype(BF16)

    @pl.when(valid)
    def _():
        x = xb_ref[...]
        gate = _dot(x, wg_ref[0])
        up = _dot(x, wu_ref[0])
        acc_ref[...] += _dot((jax.nn.silu(gate) * up).astype(BF16), wd_ref[0])

    @pl.when(f == pl.num_programs(1) - 1)
    def _():
        y_ref[...] = acc_ref[...]


def _grouped_swiglu(xs, tile_expert, tile_valid, w_gate, w_up, w_down):
    n_rows = xs.shape[0]
    d_ff = w_gate.shape[2]
    tm, tf = TM_MOE, TF_MOE
    grid_spec = pltpu.PrefetchScalarGridSpec(
        num_scalar_prefetch=2,
        grid=(n_rows // tm, d_ff // tf),
        in_specs=[pl.BlockSpec((tm, D_MODEL), lambda c, f, te, tv: (c, 0)),
                  pl.BlockSpec((1, D_MODEL, tf), lambda c, f, te, tv: (te[c], 0, f)),
                  pl.BlockSpec((1, D_MODEL, tf), lambda c, f, te, tv: (te[c], 0, f)),
                  pl.BlockSpec((1, tf, D_MODEL), lambda c, f, te, tv: (te[c], f, 0))],
        out_specs=pl.BlockSpec((tm, D_MODEL), lambda c, f, te, tv: (c, 0)),
        scratch_shapes=[pltpu.VMEM((tm, D_MODEL), BF16), pltpu.VMEM((tm, D_MODEL), F32)],
    )
    return pl.pallas_call(
        _gmm_kernel,
        grid_spec=grid_spec,
        out_shape=jax.ShapeDtypeStruct((n_rows, D_MODEL), F32),
        compiler_params=_cparams(("parallel", "arbitrary")),
        name="moe_grouped_swiglu",
    )(tile_expert, tile_valid, xs, w_gate.astype(BF16), w_up.astype(BF16), w_down.astype(BF16))


def _combine_kernel(dest_ref, y_ref, route_ref, h_ref, o_ref, buf_ref, sem):
    tc = h_ref.shape[0]

    def issue(i, carry):
        for k in range(TOP_K):
            _row_copy(y_ref, dest_ref[0, 0, TOP_K * i + k], buf_ref.at[k], i, sem).start()
        return carry

    lax.fori_loop(0, tc, issue, 0)
    for k in range(TOP_K):
        pltpu.make_async_copy(y_ref.at[pl.ds(0, tc)], buf_ref.at[k], sem).wait()
    g = route_ref[...]
    o_ref[...] = h_ref[...] + g[:, 2:3] * buf_ref[0] + g[:, 3:4] * buf_ref[1]


def _combine(y, dest, route, h):
    n = h.shape[0]
    tc = COMBINE_TOKENS
    return pl.pallas_call(
        _combine_kernel,
        grid=(n // tc,),
        in_specs=[pl.BlockSpec((1, 1, tc * TOP_K), lambda i: (i, 0, 0), memory_space=pltpu.SMEM),
                  pl.BlockSpec(memory_space=pl.ANY),
                  pl.BlockSpec((tc, LANES), lambda i: (i, 0)),
                  pl.BlockSpec((tc, D_MODEL), lambda i: (i, 0))],
        out_specs=pl.BlockSpec((tc, D_MODEL), lambda i: (i, 0)),
        out_shape=jax.ShapeDtypeStruct((n, D_MODEL), F32),
        scratch_shapes=[pltpu.VMEM((TOP_K, tc, D_MODEL), F32), pltpu.SemaphoreType.DMA(())],
        compiler_params=_cparams(("arbitrary",)),
        name="moe_combine",
    )(dest.reshape(n // tc, 1, tc * TOP_K), y, route, h)


def _moe(v, route, h, w_gate, w_up, w_down):
    n = h.shape[0]
    nk = n * TOP_K
    tm = TM_MOE
    e_flat = route[:, :TOP_K].astype(jnp.int32).reshape(nk)
    onehot = (e_flat[:, None] == jnp.arange(N_EXPERTS)[None, :]).astype(jnp.int32)
    csum = jnp.cumsum(onehot, axis=0)
    counts = csum[-1]
    rank = jnp.sum(jnp.where(onehot > 0, csum, 0), axis=1) - 1
    padded = (counts + tm - 1) // tm * tm
    pend = jnp.cumsum(padded)
    pstart = pend - padded
    dest = (jnp.sum(jnp.where(onehot > 0, pstart[None, :], 0), axis=1) + rank).astype(jnp.int32)
    n_tiles = nk // tm + N_EXPERTS
    tile_start = jnp.arange(n_tiles) * tm
    tile_expert = jnp.minimum(jnp.sum(tile_start[:, None] >= pend[None, :], axis=1), N_EXPERTS - 1)
    tile_valid = (tile_start < pend[-1]).astype(jnp.int32)
    xs = _dispatch(v, dest, n_tiles * tm)
    y = _grouped_swiglu(xs, tile_expert.astype(jnp.int32), tile_valid, w_gate, w_up, w_down)
    return _combine(y, dest, route, h)


def kernel(x, rel_bias, attn_norm, w_in, nsa_q_norm, nsa_k_norm, cmp_pos, cmp_w1, cmp_b1, cmp_w2,
           dil_q_norm, dil_k_norm, out_norm, w_out, ffn_norm, ffn_w_gate, ffn_w_up, ffn_w_down,
           router_w, exp_w_gate, exp_w_up, exp_w_down):
    batch, seq, _ = x.shape
    depth = attn_norm.shape[0]
    n = batch * seq
    assert seq % (DIL_PAIRS[-1][1] * DIL_ROWS) == 0 and n % DISPATCH_TOKENS == 0
    assert all(w == d * DIL_BAND for w, d in DIL_PAIRS)

    tbl = rel_bias.astype(F32).T
    tbl_a, tbl_b = tbl[:N_HEADS_A], tbl[N_HEADS_A:]
    max_off = seq // TOEP - 1
    far_sel = min(max_off, -(-(FAR_DIST + TOEP - 1) // TOEP))
    far_win = min(max_off, -(-(WIN_A + TOEP - 1) // TOEP))
    bias_c = _bias_table(tbl_a, seq // TQ_NSA, TQ_NSA, seq // CMP_STRIDE,
                         functools.partial(_cmp_dist, n_cmp=(seq - CMP_LEN) // CMP_STRIDE + 1))
    bias_s = _bias_table(tbl_a, far_sel + 2, TOEP, TOEP, _sel_dist)
    bias_w = _bias_table(tbl_a, far_win + 2, TOEP, TOEP, _win_dist)
    bias_d = [_bias_table(tbl_b, 1, DIL_BAND, 2 * DIL_BAND, functools.partial(_dil_dist, dil=d))
              for _, d in DIL_PAIRS]

    h = x.reshape(n, D_MODEL)
    for layer in range(depth):
        qa, cva, ksw, ga, qkv_dilated = _in_projection(
            h, seq, attn_norm[layer], w_in[layer], nsa_q_norm[layer], nsa_k_norm[layer],
            dil_q_norm[layer], dil_k_norm[layer])
        kc, vc = _compress(cva, cmp_pos[layer], cmp_w1[layer], cmp_b1[layer], cmp_w2[layer],
                           nsa_k_norm[layer, 0])
        o_a = _nsa_attention(qa, kc, vc, ksw, ga, bias_c, bias_s, bias_w, batch, seq)
        dil_outs = [_dilated_attention(*qkv_dilated[i], bias_d[i], d) for i, (_, d) in enumerate(DIL_PAIRS)]
        moe_layer = layer % 2 == 1
        outs = _out_projection(o_a, dil_outs, h, out_norm[layer], w_out[layer], ffn_norm[layer],
                               router_w[layer // 2] if moe_layer else None)
        if moe_layer:
            h1, v, route = outs
            h = _moe(v, route, h1, exp_w_gate[layer // 2], exp_w_up[layer // 2], exp_w_down[layer // 2])
        else:
            h1, v = outs
            h = _dense_ffn(v, h1, ffn_w_gate[layer // 2], ffn_w_up[layer // 2], ffn_w_down[layer // 2])
    return h.reshape(batch, seq, D_MODEL)
```

```python
import functools
import math

import jax
import jax.numpy as jnp
import numpy as np
from jax import lax
from jax.experimental import pallas as pl
from jax.experimental.pallas import tpu as pltpu

F32 = jnp.float32
BF16 = jnp.bfloat16

D_MODEL = 1024
HEAD_DIM = 64
N_HEADS_A = 8
N_KV_A = 2
HPG_A = N_HEADS_A // N_KV_A
N_HEADS_B = 8
WIDTH_A = N_HEADS_A * HEAD_DIM
WIDTH_B = N_HEADS_B * HEAD_DIM
KV_A = N_KV_A * HEAD_DIM
CMP_LEN = 32
CMP_STRIDE = 16
CMP_HIDDEN = 256
SEL_BLOCK = 64
SEL_TOPK = 16
WIN_A = 512
DIL_PAIRS = ((128, 1), (512, 4), (2048, 16))
N_BUCKETS = 32
MAX_DISTANCE = 2048
N_EXPERTS = 8
TOP_K = 2
RMS_EPS = 1e-6
NEG = -1e30
SCALE = HEAD_DIM ** -0.5
LOG2E = math.log2(math.e)

LANES = 128
MXU_DIM = 256
VMEM_LIMIT = 56 * 1024 * 1024

TM_PROJ = 512
TQ_NSA = 256
TK_NSA = 512
TOEP = 128
DIL_BAND = 128
DIL_ROWS = 256
TM_FFN = 512
TF_FFN = 1408
TM_MOE = 1024
TF_MOE = 512
DISPATCH_TOKENS = 512
COMBINE_TOKENS = 256
DMA_ISSUE_UNROLL = 8

COL_QA = 0
COL_CVA = 512
COL_KSW = 768
COL_QB = 1280
COL_KB = 1792
COL_VB = 2304
COL_GA = 2816
IN_COLS = 2944
PAIRED_HEADS_A = np.array([h + g * HPG_A for h in range(HPG_A) for g in range(N_KV_A)])


def _cparams(sem, vmem=VMEM_LIMIT):
    return pltpu.CompilerParams(dimension_semantics=sem, vmem_limit_bytes=vmem)


def _dot(a, b):
    return jnp.dot(a, b, preferred_element_type=F32)


def _dot_nt(a, b):
    return lax.dot_general(a, b, (((1,), (1,)), ((), ())), preferred_element_type=F32)


def _split_bf16(x):
    hi = x.astype(BF16)
    lo = (x - hi.astype(F32)).astype(BF16)
    return hi, lo


def _bucket_thresholds():
    d = np.arange(0, 4 * MAX_DISTANCE, dtype=np.int64)
    max_exact = N_BUCKETS // 2
    scaled = np.log(np.maximum(d, 1).astype(np.float32) / np.float32(max_exact)) / np.float32(
        math.log(MAX_DISTANCE / max_exact))
    large = np.minimum(max_exact + (scaled.astype(np.float32) * (N_BUCKETS - max_exact)).astype(np.int32),
                       N_BUCKETS - 1)
    bucket = np.where(d < max_exact, d, large)
    assert np.all(np.diff(bucket) >= 0)
    return [int(np.argmax(bucket >= b)) for b in range(N_BUCKETS)]


_THR = _bucket_thresholds()
FAR_DIST = _THR[N_BUCKETS - 1]


def _bias_table_kernel(tbl_ref, out_ref, *, n_heads, rows, cols, dist_valid):
    i = pl.program_id(0)
    a = lax.broadcasted_iota(jnp.int32, (rows, cols), 0)
    c = lax.broadcasted_iota(jnp.int32, (rows, cols), 1)
    d, valid = dist_valid(i, a, c)
    for h in range(n_heads):
        acc = jnp.full((rows, cols), tbl_ref[h, 0], F32)
        for b in range(1, N_BUCKETS):
            acc = jnp.where(d >= _THR[b], tbl_ref[h, b], acc)
        out_ref[h, 0] = jnp.where(valid, acc, NEG)


def _bias_table(tbl, n_tiles, rows, cols, dist_valid):
    n_heads = tbl.shape[0]
    return pl.pallas_call(
        functools.partial(_bias_table_kernel, n_heads=n_heads, rows=rows, cols=cols, dist_valid=dist_valid),
        grid=(n_tiles,),
        in_specs=[pl.BlockSpec(memory_space=pltpu.SMEM)],
        out_specs=pl.BlockSpec((n_heads, 1, rows, cols), lambda i: (0, i, 0, 0)),
        out_shape=jax.ShapeDtypeStruct((n_heads, n_tiles, rows, cols), F32),
        compiler_params=_cparams(("arbitrary",)),
        name="bias_table",
    )(tbl)


def _cmp_dist(i, a, c, *, n_cmp):
    d = i * TQ_NSA + a - (c * CMP_STRIDE + CMP_LEN - 1)
    return d, (d >= 0) & (c < n_cmp)


def _sel_dist(i, a, c):
    d = (i - 1) * TOEP + a - c
    return d, d >= 0


def _win_dist(i, a, c):
    d = (i - 1) * TOEP + a - c
    return d, (d >= 0) & (d < WIN_A)


def _dil_dist(i, a, c, *, dil):
    n = DIL_BAND + a - c
    return n * dil, (n >= 0) & (n <= DIL_BAND)


def _inproj_kernel(h_ref, gn_ref, w_ref, gain_ref, bd_ref,
                   qa_ref, cva_ref, ksw_ref, ga_ref, *rest):
    dil_refs, stage_ref = rest[:-1], rest[-1]
    tm = h_ref.shape[0]

    def emit_dilated(y, which):
        dil_refs[which][0, 0] = y.astype(BF16)
        n_chunks = y.shape[1] // LANES
        for c in range(n_chunks):
            stage_ref[c] = y[:, c * LANES:(c + 1) * LANES]
        for d, (_, r) in enumerate(DIL_PAIRS):
            if r == 1:
                continue
            ref = dil_refs[3 * d + which]
            for rho in range(r):
                for c in range(n_chunks):
                    ref[0, rho, :, c * LANES:(c + 1) * LANES] = stage_ref[
                        c, pl.ds(rho, tm // r, stride=r), :].astype(BF16)

    x = h_ref[...]
    u = (x * lax.rsqrt(jnp.mean(x * x, axis=-1, keepdims=True) + RMS_EPS) * gn_ref[...]).astype(BF16)

    def proj(c0, width):
        return _dot(u, w_ref[:, c0:c0 + width])

    def headnorm(acc, c0):
        outs = []
        for j in range(acc.shape[1] // MXU_DIM):
            a = acc[:, j * MXU_DIM:(j + 1) * MXU_DIM]
            sq_hi, sq_lo = _split_bf16(a * a)
            ms = _dot(sq_hi, bd_ref[...]) + _dot(sq_lo, bd_ref[...])
            g = gain_ref[:, c0 + j * MXU_DIM:c0 + (j + 1) * MXU_DIM]
            outs.append(a * lax.rsqrt(ms + RMS_EPS) * g)
        return outs[0] if len(outs) == 1 else jnp.concatenate(outs, axis=1)

    qa_ref[...] = headnorm(proj(COL_QA, WIDTH_A), COL_QA).astype(BF16)
    cva = proj(COL_CVA, 2 * KV_A)
    for c in range(2 * KV_A // LANES):
        stage_ref[c] = cva[:, c * LANES:(c + 1) * LANES]
    for l in range(CMP_STRIDE):
        for c in range(2 * KV_A // LANES):
            cva_ref[0, :, l * 2 * KV_A + c * LANES:l * 2 * KV_A + (c + 1) * LANES] = stage_ref[
                c, pl.ds(l, tm // CMP_STRIDE, stride=CMP_STRIDE), :]
    ksw = proj(COL_KSW, 4 * KV_A)
    ksw_ref[:, :2 * KV_A] = headnorm(ksw[:, :2 * KV_A], COL_KSW).astype(BF16)
    ksw_ref[:, 2 * KV_A:] = ksw[:, 2 * KV_A:].astype(BF16)
    emit_dilated(headnorm(proj(COL_QB, WIDTH_B), COL_QB), 0)
    emit_dilated(headnorm(proj(COL_KB, WIDTH_B), COL_KB), 1)
    emit_dilated(proj(COL_VB, WIDTH_B), 2)
    ga_ref[...] = jax.nn.sigmoid(proj(COL_GA, LANES))


def _in_projection(h, seq, attn_norm, w_in, nsa_q_norm, nsa_k_norm, dil_q_norm, dil_k_norm):
    n = h.shape[0]
    o = np.cumsum((0, WIDTH_A, KV_A, KV_A, KV_A, KV_A, KV_A, KV_A, N_HEADS_A * 3, WIDTH_B, WIDTH_B, WIDTH_B))
    seg = [w_in[:, o[i]:o[i + 1]] for i in range(11)]
    qa, kc, vc, ks, vs, kw, vw, ga, qb, kb, vb = seg
    qa = qa.reshape(D_MODEL, N_HEADS_A, HEAD_DIM)[:, PAIRED_HEADS_A, :].reshape(D_MODEL, WIDTH_A)
    pad = jnp.zeros((D_MODEL, IN_COLS - COL_GA - N_HEADS_A * 3), w_in.dtype)
    w = jnp.concatenate([qa, kc, vc, ks, kw, vs, vw, qb, kb, vb, ga, pad], axis=1).astype(BF16)
    ones = jnp.ones((IN_COLS,), F32)
    gain = ones
    gain = gain.at[COL_QA:COL_QA + WIDTH_A].set(jnp.tile(nsa_q_norm, N_HEADS_A) * (SCALE * LOG2E))
    gain = gain.at[COL_KSW:COL_KSW + KV_A].set(jnp.tile(nsa_k_norm[1], N_KV_A))
    gain = gain.at[COL_KSW + KV_A:COL_KSW + 2 * KV_A].set(jnp.tile(nsa_k_norm[2], N_KV_A))
    gain = gain.at[COL_QB:COL_QB + WIDTH_B].set(jnp.tile(dil_q_norm, N_HEADS_B) * (SCALE * LOG2E))
    gain = gain.at[COL_KB:COL_KB + WIDTH_B].set(jnp.tile(dil_k_norm, N_HEADS_B))
    blk = np.arange(MXU_DIM) // HEAD_DIM
    bd = jnp.asarray((blk[:, None] == blk[None, :]).astype(np.float32) / HEAD_DIM, BF16)

    tm = TM_PROJ
    tiles_per_seq = seq // tm
    row = lambda width: pl.BlockSpec((tm, width), lambda i: (i, 0))
    full = lambda a: pl.BlockSpec(a.shape, lambda i: (0,) * a.ndim)
    gn = attn_norm.reshape(1, D_MODEL)
    gain = gain.reshape(1, IN_COLS)
    dil_specs, dil_shapes = [], []
    for _, r in DIL_PAIRS:
        spec = pl.BlockSpec((1, r, tm // r, WIDTH_B), lambda i: (i // tiles_per_seq, 0, i % tiles_per_seq, 0))
        dil_specs += [spec] * 3
        dil_shapes += [jax.ShapeDtypeStruct((n // seq, r, seq // r, WIDTH_B), BF16)] * 3
    outs = pl.pallas_call(
        _inproj_kernel,
        grid=(n // tm,),
        in_specs=[row(D_MODEL), full(gn), full(w), full(gain), full(bd)],
        out_specs=[row(WIDTH_A),
                   pl.BlockSpec((1, tm // CMP_STRIDE, CMP_STRIDE * 2 * KV_A),
                                lambda i: (i // tiles_per_seq, i % tiles_per_seq, 0)),
                   row(4 * KV_A), row(LANES)] + dil_specs,
        out_shape=[jax.ShapeDtypeStruct((n, WIDTH_A), BF16),
                   jax.ShapeDtypeStruct((n // seq, seq // CMP_STRIDE, CMP_STRIDE * 2 * KV_A), F32),
                   jax.ShapeDtypeStruct((n, 4 * KV_A), BF16), jax.ShapeDtypeStruct((n, LANES), F32)] + dil_shapes,
        scratch_shapes=[pltpu.VMEM((WIDTH_B // LANES, tm, LANES), F32)],
        compiler_params=_cparams(("parallel",)),
        name="in_projection",
    )(h, gn, w, gain, bd)
    qa, cva, ksw, ga = outs[:4]
    qkv_dilated = [outs[4 + 3 * d:7 + 3 * d] for d in range(len(DIL_PAIRS))]
    return qa, cva, ksw, ga, qkv_dilated


def _gelu_tanh(x):
    return 0.5 * x * (1.0 + jnp.tanh(math.sqrt(2.0 / math.pi) * (x + 0.044715 * (x * x * x))))


def _compress_kernel(x_ref, pos_ref, w1_ref, b1_ref, w2_ref, kg_ref, kc_ref, vc_ref):
    rows = x_ref.shape[1]
    half = CMP_LEN // 2
    for which, out_ref in ((0, kc_ref), (1, vc_ref)):
        top = jnp.zeros((rows, 2 * CMP_HIDDEN), F32)
        bot = jnp.zeros((rows, 2 * CMP_HIDDEN), F32)
        for l in range(half):
            c0 = l * 2 * KV_A + which * KV_A
            a = x_ref[0, :, c0:c0 + KV_A]
            top += _dot((a + pos_ref[which, l:l + 1, :]).astype(BF16), w1_ref[which, l])
            bot += _dot((a + pos_ref[which, half + l:half + l + 1, :]).astype(BF16), w1_ref[which, half + l])
        hid = top + pltpu.roll(bot, rows - 1, axis=0) + b1_ref[which]
        y = _dot(_gelu_tanh(hid).astype(BF16), w2_ref[which])
        if which == 0:
            parts = []
            for g in range(N_KV_A):
                yg = y[:, g * HEAD_DIM:(g + 1) * HEAD_DIM]
                parts.append(yg * lax.rsqrt(jnp.mean(yg * yg, axis=-1, keepdims=True) + RMS_EPS))
            y = jnp.concatenate(parts, axis=1) * kg_ref[...]
        out_ref[0] = y.astype(BF16)


def _compress(x, cmp_pos, cmp_w1, cmp_b1, cmp_w2, k_norm0):
    batch, rows, _ = x.shape
    pos = jnp.tile(cmp_pos, (1, 1, N_KV_A))
    w1 = cmp_w1.reshape(2, CMP_LEN, HEAD_DIM, CMP_HIDDEN).astype(BF16)
    z1 = jnp.zeros_like(w1)
    w1 = jnp.concatenate([jnp.concatenate([w1, z1], axis=3), jnp.concatenate([z1, w1], axis=3)], axis=2)
    b1 = jnp.tile(cmp_b1, (1, N_KV_A)).reshape(2, 1, 2 * CMP_HIDDEN)
    w2 = cmp_w2.astype(BF16)
    z2 = jnp.zeros_like(w2)
    w2 = jnp.concatenate([jnp.concatenate([w2, z2], axis=2), jnp.concatenate([z2, w2], axis=2)], axis=1)
    kg = jnp.tile(k_norm0, N_KV_A).reshape(1, KV_A)
    full = lambda a: pl.BlockSpec(a.shape, lambda b: (0,) * a.ndim)
    out = pl.BlockSpec((1, rows, KV_A), lambda b: (b, 0, 0))
    return pl.pallas_call(
        _compress_kernel,
        grid=(batch,),
        in_specs=[pl.BlockSpec((1, rows, x.shape[2]), lambda b: (b, 0, 0)),
                  full(pos), full(w1), full(b1), full(w2), full(kg)],
        out_specs=[out, out],
        out_shape=[jax.ShapeDtypeStruct((batch, rows, KV_A), BF16)] * 2,
        compiler_params=_cparams(("parallel",)),
        name="nsa_compress",
    )(x, pos, w1, b1, w2, kg)


def _toeplitz_bias(tbl_ref, g, base, n_a, n_c):
    far = tbl_ref.shape[1] - 2
    rows = []
    for a in range(n_a):
        tiles = [tbl_ref[g * HPG_A:(g + 1) * HPG_A, jnp.clip(base + a - c, -1, far) + 1] for c in range(n_c)]
        rows.append(jnp.concatenate(tiles, axis=2))
    return jnp.concatenate(rows, axis=1)


def _nsa_kernel(q_ref, kc_ref, vc_ref, ksw_ref, ga_ref, bc_ref, bs_ref, bw_ref, ex_ref, gp_ref, o_ref,
                imp_ref, *, seq):
    tq, tk = TQ_NSA, TK_NSA
    qi = pl.program_id(1)
    rows_c = seq // CMP_STRIDE
    n_sb = seq // SEL_BLOCK
    k_sel = min(SEL_TOPK, n_sb)
    rows = HPG_A * tq
    groups = range(N_KV_A)

    lane = lax.broadcasted_iota(jnp.int32, (1, LANES), 1)
    low_half = lane < HEAD_DIM
    half_bf = [jnp.where(low_half, 1.0, 0.0).astype(BF16), jnp.where(low_half, 0.0, 1.0).astype(BF16)]
    qs = [jnp.concatenate([q_ref[:, c * LANES:(c + 1) * LANES] * half_bf[g] for c in range(HPG_A)], axis=0)
          for g in groups]

    def row_sums(p_bf):
        return _dot(p_bf, jnp.ones((p_bf.shape[1], LANES), BF16))

    jj = lax.broadcasted_iota(jnp.int32, (n_sb, rows_c), 0) * SEL_BLOCK
    nn = lax.broadcasted_iota(jnp.int32, (n_sb, rows_c), 1) * CMP_STRIDE
    ov = jnp.maximum(jnp.minimum(nn + CMP_LEN, jj + SEL_BLOCK) - jnp.maximum(nn, jj), 0)
    ov_t = (ov.astype(F32) * (1.0 / CMP_LEN)).astype(BF16)
    blk = lax.broadcasted_iota(jnp.int32, (n_sb, tq), 0)
    tpos = qi * tq + lax.broadcasted_iota(jnp.int32, (n_sb, tq), 1)
    cur = tpos // SEL_BLOCK
    forced = (blk == 0) | (blk == cur) | (blk == cur - 1)
    future = blk * SEL_BLOCK > tpos

    o_c, sel_bf = [], []
    for g in groups:
        s = _dot_nt(qs[g], kc_ref[0]).reshape(HPG_A, tq, rows_c) + bc_ref[g * HPG_A:(g + 1) * HPG_A, 0]
        m = jnp.max(s, axis=-1, keepdims=True)
        e = jnp.where(s > 0.5 * NEG, jnp.exp2(s - m), 0.0)
        e_bf = e.reshape(rows, rows_c).astype(BF16)
        inv = 1.0 / jnp.maximum(row_sums(e_bf), 1e-30)
        o_c.append(_dot(e_bf, vc_ref[0]) * inv)
        inv4 = inv.reshape(HPG_A, tq, LANES)
        p = e * jnp.concatenate([inv4] * (rows_c // LANES), axis=2)
        p_sum = p[0] + p[1] + p[2] + p[3]
        p_hi, p_lo = _split_bf16(p_sum)
        p_lo2 = (p_sum - p_hi.astype(F32) - p_lo.astype(F32)).astype(BF16)
        imp = _dot_nt(ov_t, p_hi) + _dot_nt(ov_t, p_lo) + _dot_nt(ov_t, p_lo2)
        imp = jnp.where(forced, 1e6, jnp.where(future, -1e6, imp))
        imp_ref[...] = imp

        def count_beaten(i0, cnt):
            for u in range(tq // SEL_BLOCK):
                i = i0 * (tq // SEL_BLOCK) + u
                ri = jnp.broadcast_to(imp_ref[pl.ds(i, 1), :], (n_sb, tq))
                later = jnp.where(blk > i, 1.0, 0.0)
                cnt = cnt + jnp.where(ri > imp, 1.0, jnp.where(ri == imp, later, 0.0))
            return cnt

        cnt = lax.fori_loop(0, qi + 1, count_beaten, jnp.zeros((n_sb, tq), F32))
        sel_bf.append(jnp.where(cnt < k_sel, 1.0, 0.0).T.astype(BF16))

    def with_ones(v_pair, g):
        return v_pair * half_bf[g] + half_bf[1 - g]

    def normalised(acc):
        return acc / pltpu.roll(acc, HEAD_DIM, axis=1)

    def sel_body(kj, carry):
        r0 = pl.multiple_of(kj * tk, tk)
        k = ksw_ref[pl.ds(r0, tk), 0:KV_A]
        v = ksw_ref[pl.ds(r0, tk), 2 * KV_A:3 * KV_A]
        out = []
        for g in groups:
            m, acc = carry[g]
            s = _dot_nt(qs[g], k).reshape(HPG_A, tq, tk)
            s = s + _toeplitz_bias(bs_ref, g, (tq // TOEP) * qi - (tk // TOEP) * kj, tq // TOEP, tk // TOEP)
            madd = (_dot(sel_bf[g], ex_ref[kj]) - 1.0) * (-NEG)
            s = (s + madd[None]).reshape(rows, tk)
            m_new = jnp.maximum(m, jnp.max(s, axis=-1, keepdims=True))
            p = jnp.exp2(s - m_new).astype(BF16)
            out.append((m_new, jnp.exp2(m - m_new) * acc + _dot(p, with_ones(v, g))))
        return tuple(out)

    init = tuple((jnp.full((rows, 1), NEG, F32), jnp.zeros((rows, LANES), F32)) for _ in groups)
    n_tiles = (qi * tq + tq + tk - 1) // tk
    sel_out = lax.fori_loop(0, n_tiles, sel_body, init)
    o_s = [normalised(acc) for _, acc in sel_out]

    n_wk = WIN_A + tq
    start = pl.multiple_of(jnp.maximum(qi * tq - WIN_A, 0), tq)
    kw = ksw_ref[pl.ds(start, n_wk), KV_A:2 * KV_A]
    vw = ksw_ref[pl.ds(start, n_wk), 3 * KV_A:4 * KV_A]
    o_w = []
    for g in groups:
        s = _dot_nt(qs[g], kw).reshape(HPG_A, tq, n_wk)
        s = s + _toeplitz_bias(bw_ref, g, (qi * tq - start) // TOEP, tq // TOEP, n_wk // TOEP)
        s = s.reshape(rows, n_wk)
        p = jnp.exp2(s - jnp.max(s, axis=-1, keepdims=True)).astype(BF16)
        o_w.append(normalised(_dot(p, with_ones(vw, g))))

    g_hi, g_lo = _split_bf16(ga_ref[...])
    gates = _dot(g_hi, gp_ref[...]) + _dot(g_lo, gp_ref[...])
    for c in range(HPG_A):
        rs = slice(c * tq, (c + 1) * tq)
        out = jnp.zeros((tq, LANES), F32)
        for j, o in enumerate((o_c, o_s, o_w)):
            gate = gates[:, j * WIDTH_A + c * LANES:j * WIDTH_A + (c + 1) * LANES]
            out = out + gate * jnp.where(low_half, o[0][rs], o[1][rs])
        o_ref[:, c * LANES:(c + 1) * LANES] = out


def _nsa_attention(qa, kc, vc, ksw, ga, bias_c, bias_s, bias_w, batch, seq):
    tq, tk = TQ_NSA, TK_NSA
    nq = seq // tq
    n = batch * seq
    rows_c = seq // CMP_STRIDE
    n_sb = seq // SEL_BLOCK
    assert rows_c % LANES == 0 and seq % tk == 0 and seq >= WIN_A + tq and WIN_A % tq == 0
    key_blk = (np.arange(seq) // SEL_BLOCK).reshape(seq // tk, 1, tk)
    expand = jnp.asarray((key_blk == np.arange(n_sb).reshape(1, n_sb, 1)).astype(np.float32), BF16)
    gp = np.zeros((LANES, 3 * WIDTH_A), np.float32)
    for pos, head in enumerate(PAIRED_HEADS_A):
        for j in range(3):
            gp[3 * head + j, j * WIDTH_A + pos * HEAD_DIM:j * WIDTH_A + (pos + 1) * HEAD_DIM] = 1.0
    gp = jnp.asarray(gp, BF16)
    resident = lambda a: pl.BlockSpec(a.shape, lambda b, i: (0,) * a.ndim, pipeline_mode=pl.Buffered(1))
    return pl.pallas_call(
        functools.partial(_nsa_kernel, seq=seq),
        grid=(batch, nq),
        in_specs=[
            pl.BlockSpec((tq, WIDTH_A), lambda b, i: (b * nq + i, 0)),
            pl.BlockSpec((1, rows_c, KV_A), lambda b, i: (b, 0, 0)),
            pl.BlockSpec((1, rows_c, KV_A), lambda b, i: (b, 0, 0)),
            pl.BlockSpec((seq, 4 * KV_A), lambda b, i: (b, 0)),
            pl.BlockSpec((tq, LANES), lambda b, i: (b * nq + i, 0)),
            pl.BlockSpec((N_HEADS_A, 1, tq, rows_c), lambda b, i: (0, i, 0, 0)),
            resident(bias_s), resident(bias_w), resident(expand), resident(gp),
        ],
        out_specs=pl.BlockSpec((tq, WIDTH_A), lambda b, i: (b * nq + i, 0)),
        out_shape=jax.ShapeDtypeStruct((n, WIDTH_A), F32),
        scratch_shapes=[pltpu.VMEM((n_sb, tq), F32)],
        compiler_params=_cparams(("parallel", "arbitrary")),
        name="nsa_attention",
    )(qa, kc, vc, ksw, ga, bias_c, bias_s, bias_w, expand, gp)


def _dilated_kernel(q_ref, kp_ref, kc_ref, vp_ref, vc_ref, bias_ref, o_ref, lse_ref):
    band = DIL_BAND
    first = pl.program_id(2) == 0
    prev_mask = jnp.where(first, NEG, 0.0)
    lane = lax.broadcasted_iota(jnp.int32, (1, LANES), 1)
    low_half = lane < HEAD_DIM
    half_bf = [jnp.where(low_half, 1.0, 0.0).astype(BF16), jnp.where(low_half, 0.0, 1.0).astype(BF16)]
    den_sel = [jnp.broadcast_to(hm, (2 * band, LANES)) for hm in half_bf]
    key_col = lax.broadcasted_iota(jnp.int32, (1, 2 * band), 1)
    prev_cols = jnp.where(key_col < band, prev_mask, 0.0)
    for sub in range(q_ref.shape[0] // band):
        r_cur = slice(sub * band, (sub + 1) * band)
        for c in range(WIDTH_B // LANES):
            cols = slice(c * LANES, (c + 1) * LANES)
            q = q_ref[r_cur, cols]
            if sub == 0:
                k_cat = jnp.concatenate([kp_ref[:, cols], kc_ref[r_cur, cols]], axis=0)
                v_cat = jnp.concatenate([vp_ref[:, cols], vc_ref[r_cur, cols]], axis=0)
            else:
                k_cat = kc_ref[(sub - 1) * band:(sub + 1) * band, cols]
                v_cat = vc_ref[(sub - 1) * band:(sub + 1) * band, cols]
            acc = jnp.zeros((band, 2 * LANES), F32)
            ms = []
            for j in range(2):
                h = 2 * c + j
                s = _dot_nt(q * half_bf[j], k_cat) + bias_ref[h, 0]
                if sub == 0:
                    s = s + prev_cols
                m = jnp.max(s, axis=-1, keepdims=True)
                e = jnp.exp2(s - m).astype(BF16)
                acc = acc + _dot(e, jnp.concatenate([v_cat * half_bf[j], den_sel[j]], axis=1))
                ms.append(m)
            den = acc[:, LANES:]
            o_ref[r_cur, cols] = acc[:, :LANES] / den
            lse_ref[r_cur, cols] = jnp.where(low_half, ms[0], ms[1]) + jnp.log2(den)


def _dilated_attention(q, k, v, bias, dil):
    band, rows = DIL_BAND, DIL_ROWS
    batch, _, length, _ = q.shape
    sub = rows // band
    cur = pl.BlockSpec((None, None, rows, WIDTH_B), lambda b, r, n: (b, r, n, 0))
    prev = pl.BlockSpec((None, None, band, WIDTH_B), lambda b, r, n: (b, r, jnp.maximum(n * sub - 1, 0), 0))
    return pl.pallas_call(
        _dilated_kernel,
        grid=(batch, dil, length // rows),
        in_specs=[cur, prev, cur, prev, cur,
                  pl.BlockSpec(bias.shape, lambda b, r, n: (0, 0, 0, 0))],
        out_specs=[cur, cur],
        out_shape=[jax.ShapeDtypeStruct((batch, dil, length, WIDTH_B), F32)] * 2,
        compiler_params=_cparams(("parallel", "parallel", "arbitrary")),
        name=f"dilated_attention_d{dil}",
    )(q, k, k, v, v, bias)


def _outproj_kernel(*refs, with_router):
    if with_router:
        (oa_ref, o1_ref, o2_ref, o3_ref, l1_ref, l2_ref, l3_ref, h_ref, gn_ref, w_ref, fg_ref, rw_ref,
         h_out_ref, v_ref, route_ref, stage_ref) = refs
    else:
        (oa_ref, o1_ref, o2_ref, o3_ref, l1_ref, l2_ref, l3_ref, h_ref, gn_ref, w_ref, fg_ref,
         h_out_ref, v_ref, stage_ref) = refs
    tm = h_ref.shape[0]

    def token_order(ref, slot):
        r = ref.shape[1]
        if r == 1:
            return ref[0, 0]
        n_chunks = ref.shape[3] // LANES
        for rho in range(r):
            for c in range(n_chunks):
                stage_ref[slot, c, pl.ds(rho, tm // r, stride=r), :] = ref[0, rho, :, c * LANES:(c + 1) * LANES]
        return jnp.concatenate([stage_ref[slot, c] for c in range(n_chunks)], axis=1)

    l1, l2, l3 = token_order(l1_ref, 0), token_order(l2_ref, 0), token_order(l3_ref, 1)
    m = jnp.maximum(jnp.maximum(l1, l2), l3)
    e1, e2, e3 = jnp.exp2(l1 - m), jnp.exp2(l2 - m), jnp.exp2(l3 - m)
    o1, o2, o3 = token_order(o1_ref, 0), token_order(o2_ref, 2), token_order(o3_ref, 3)
    ob = (e1 * o1 + e2 * o2 + e3 * o3) / (e1 + e2 + e3)

    def norm(x, g):
        return x * lax.rsqrt(jnp.mean(x * x, axis=-1, keepdims=True) + RMS_EPS) * g

    o = jnp.concatenate([norm(oa_ref[...], gn_ref[:, :WIDTH_A]), norm(ob, gn_ref[:, WIDTH_A:])], axis=1)
    h = h_ref[...] + _dot(o.astype(BF16), w_ref[...])
    h_out_ref[...] = h
    v = norm(h, fg_ref[...])
    v_ref[...] = v.astype(v_ref.dtype)
    if with_router:
        v_hi, v_lo = _split_bf16(v)
        w_hi, w_lo = _split_bf16(rw_ref[...])
        logits = _dot(v_hi, w_hi) + _dot(v_lo, w_hi) + _dot(v_hi, w_lo)
        lg = logits.T[:N_EXPERTS]
        row = lax.broadcasted_iota(jnp.int32, lg.shape, 0).astype(F32)
        m1 = jnp.max(lg, axis=0, keepdims=True)
        i1 = jnp.min(jnp.where(lg == m1, row, float(N_EXPERTS)), axis=0, keepdims=True)
        lg2 = jnp.where(row == i1, -jnp.inf, lg)
        m2 = jnp.max(lg2, axis=0, keepdims=True)
        i2 = jnp.min(jnp.where(lg2 == m2, row, float(N_EXPERTS)), axis=0, keepdims=True)
        e = jnp.exp(m2 - m1)
        g1 = 1.0 / (1.0 + e)
        g2 = e / (1.0 + e)
        route_ref[...] = jnp.where(row == 0, i1, jnp.where(
            row == 1, i2, jnp.where(row == 2, g1, jnp.where(row == 3, g2, 0.0))))


def _out_projection(o_a, dil_outs, h, out_norm, w_out, ffn_norm, router_w):
    n = h.shape[0]
    tm = TM_PROJ
    with_router = router_w is not None
    row = lambda width: pl.BlockSpec((tm, width), lambda i: (i, 0))
    full = lambda a: pl.BlockSpec(a.shape, lambda i: (0,) * a.ndim)
    (o1, l1), (o2, l2), (o3, l3) = dil_outs
    gn_a = out_norm[:WIDTH_A].reshape(N_HEADS_A, HEAD_DIM)[PAIRED_HEADS_A].reshape(WIDTH_A)
    gn = jnp.concatenate([gn_a, out_norm[WIDTH_A:]]).reshape(1, -1)
    fg = ffn_norm.reshape(1, -1)
    w_a = w_out[:WIDTH_A].reshape(N_HEADS_A, HEAD_DIM, D_MODEL)[PAIRED_HEADS_A].reshape(WIDTH_A, D_MODEL)
    w = jnp.concatenate([w_a, w_out[WIDTH_A:]], axis=0).astype(BF16)
    tiles_per_seq = o1.shape[2] // tm
    grouped = lambda a: pl.BlockSpec((1, a.shape[1], tm // a.shape[1], WIDTH_B),
                                     lambda i: (i // tiles_per_seq, 0, i % tiles_per_seq, 0))
    args = [o_a, o1, o2, o3, l1, l2, l3, h, gn, w, fg]
    in_specs = ([row(WIDTH_A)] + [grouped(a) for a in args[1:7]]
                + [row(D_MODEL), full(gn), full(w), full(fg)])
    out_specs = [row(D_MODEL), row(D_MODEL)]
    out_shape = [jax.ShapeDtypeStruct((n, D_MODEL), F32),
                 jax.ShapeDtypeStruct((n, D_MODEL), F32 if with_router else BF16)]
    if with_router:
        rw = jnp.pad(router_w, ((0, 0), (0, LANES - N_EXPERTS)))
        args.append(rw)
        in_specs.append(full(rw))
        out_specs.append(pl.BlockSpec((N_EXPERTS, tm), lambda i: (0, i)))
        out_shape.append(jax.ShapeDtypeStruct((N_EXPERTS, n), F32))
    return pl.pallas_call(
        functools.partial(_outproj_kernel, with_router=with_router),
        grid=(n // tm,),
        in_specs=in_specs,
        out_specs=out_specs,
        out_shape=out_shape,
        scratch_shapes=[pltpu.VMEM((4, WIDTH_B // LANES, tm, LANES), F32)],
        compiler_params=_cparams(("parallel",)),
        name="out_projection",
    )(*args)


def _ffn_kernel(v_ref, h_ref, wg_ref, wu_ref, wd_ref, o_ref, acc_ref):
    f = pl.program_id(1)

    @pl.when(f == 0)
    def _():
        acc_ref[...] = jnp.zeros_like(acc_ref)

    v = v_ref[...]
    gate = _dot(v, wg_ref[...])
    up = _dot(v, wu_ref[...])
    acc_ref[...] += _dot((jax.nn.silu(gate) * up).astype(BF16), wd_ref[...])

    @pl.when(f == pl.num_programs(1) - 1)
    def _():
        o_ref[...] = h_ref[...] + acc_ref[...]


def _dense_ffn(v, h, w_gate, w_up, w_down):
    n = h.shape[0]
    d_ff = w_gate.shape[1]
    tm, tf = TM_FFN, TF_FFN
    return pl.pallas_call(
        _ffn_kernel,
        grid=(n // tm, d_ff // tf),
        in_specs=[pl.BlockSpec((tm, D_MODEL), lambda i, f: (i, 0)),
                  pl.BlockSpec((tm, D_MODEL), lambda i, f: (i, 0)),
                  pl.BlockSpec((D_MODEL, tf), lambda i, f: (0, f)),
                  pl.BlockSpec((D_MODEL, tf), lambda i, f: (0, f)),
                  pl.BlockSpec((tf, D_MODEL), lambda i, f: (f, 0))],
        out_specs=pl.BlockSpec((tm, D_MODEL), lambda i, f: (i, 0)),
        out_shape=jax.ShapeDtypeStruct((n, D_MODEL), F32),
        scratch_shapes=[pltpu.VMEM((tm, D_MODEL), F32)],
        compiler_params=_cparams(("parallel", "arbitrary")),
        name="dense_ffn",
    )(v, h, w_gate.astype(BF16), w_up.astype(BF16), w_down.astype(BF16))


def _row_copy(src_ref, src_row, dst_ref, dst_row, sem):
    return pltpu.make_async_copy(src_ref.at[pl.ds(src_row, 1)], dst_ref.at[pl.ds(dst_row, 1)], sem)


def _dispatch_kernel(dest_ref, v_ref, xs_in_ref, xs_ref, sem):
    del xs_in_ref
    n_slots = dest_ref.shape[2]

    def issue(i, carry):
        _row_copy(v_ref, i // TOP_K, xs_ref, dest_ref[0, 0, i], sem).start()
        return carry

    lax.fori_loop(0, n_slots, issue, 0, unroll=DMA_ISSUE_UNROLL)
    pltpu.make_async_copy(xs_ref.at[pl.ds(0, n_slots)], xs_ref.at[pl.ds(0, n_slots)], sem).wait()


def _dispatch(v, dest, n_rows):
    n = v.shape[0]
    tokens = DISPATCH_TOKENS
    slots = tokens * TOP_K
    return pl.pallas_call(
        _dispatch_kernel,
        grid=(n // tokens,),
        in_specs=[pl.BlockSpec((1, 1, slots), lambda i: (i, 0, 0), memory_space=pltpu.SMEM),
                  pl.BlockSpec((tokens, D_MODEL), lambda i: (i, 0)),
                  pl.BlockSpec(memory_space=pl.ANY)],
        out_specs=pl.BlockSpec(memory_space=pl.ANY),
        out_shape=jax.ShapeDtypeStruct((n_rows, D_MODEL), F32),
        scratch_shapes=[pltpu.SemaphoreType.DMA(())],
        input_output_aliases={2: 0},
        compiler_params=_cparams(("arbitrary",)),
        name="moe_dispatch",
    )(dest.reshape(n // tokens, 1, slots), v, jnp.zeros((n_rows, D_MODEL), F32))


def _gmm_kernel(te_ref, tv_ref, x_ref, wg_ref, wu_ref, wd_ref, y_ref, xb_ref, acc_ref):
    c = pl.program_id(0)
    f = pl.program_id(1)
    valid = tv_ref[c] > 0

    @pl.when(f == 0)
    def _():
        acc_ref[...] = jnp.zeros_like(acc_ref)
        xb_ref[...] = x_ref[...].astype(BF16)

    @pl.when(valid)
    def _():
        x = xb_ref[...]
        gate = _dot(x, wg_ref[0])
        up = _dot(x, wu_ref[0])
        acc_ref[...] += _dot((jax.nn.silu(gate) * up).astype(BF16), wd_ref[0])

    @pl.when(f == pl.num_programs(1) - 1)
    def _():
        y_ref[...] = acc_ref[...]


def _grouped_swiglu(xs, tile_expert, tile_valid, w_gate, w_up, w_down):
    n_rows = xs.shape[0]
    d_ff = w_gate.shape[2]
    tm, tf = TM_MOE, TF_MOE
    grid_spec = pltpu.PrefetchScalarGridSpec(
        num_scalar_prefetch=2,
        grid=(n_rows // tm, d_ff // tf),
        in_specs=[pl.BlockSpec((tm, D_MODEL), lambda c, f, te, tv: (c, 0)),
                  pl.BlockSpec((1, D_MODEL, tf), lambda c, f, te, tv: (te[c], 0, f)),
                  pl.BlockSpec((1, D_MODEL, tf), lambda c, f, te, tv: (te[c], 0, f)),
                  pl.BlockSpec((1, tf, D_MODEL), lambda c, f, te, tv: (te[c], f, 0))],
        out_specs=pl.BlockSpec((tm, D_MODEL), lambda c, f, te, tv: (c, 0)),
        scratch_shapes=[pltpu.VMEM((tm, D_MODEL), BF16), pltpu.VMEM((tm, D_MODEL), F32)],
    )
    return pl.pallas_call(
        _gmm_kernel,
        grid_spec=grid_spec,
        out_shape=jax.ShapeDtypeStruct((n_rows, D_MODEL), F32),
        compiler_params=_cparams(("parallel", "arbitrary")),
        name="moe_grouped_swiglu",
    )(tile_expert, tile_valid, xs, w_gate.astype(BF16), w_up.astype(BF16), w_down.astype(BF16))


def _combine_kernel(dest_ref, y_ref, gate_ref, h_ref, o_ref, buf_ref, sem):
    tc = h_ref.shape[0]

    def issue(i, carry):
        for k in range(TOP_K):
            _row_copy(y_ref, dest_ref[0, 0, TOP_K * i + k], buf_ref.at[k], i, sem).start()
        return carry

    lax.fori_loop(0, tc, issue, 0, unroll=DMA_ISSUE_UNROLL)
    for k in range(TOP_K):
        pltpu.make_async_copy(y_ref.at[pl.ds(0, tc)], buf_ref.at[k], sem).wait()
    g = gate_ref[...]
    o_ref[...] = h_ref[...] + g[:, 0:1] * buf_ref[0] + g[:, 1:2] * buf_ref[1]


def _combine(y, dest, gates, h):
    n = h.shape[0]
    tc = COMBINE_TOKENS
    return pl.pallas_call(
        _combine_kernel,
        grid=(n // tc,),
        in_specs=[pl.BlockSpec((1, 1, tc * TOP_K), lambda i: (i, 0, 0), memory_space=pltpu.SMEM),
                  pl.BlockSpec(memory_space=pl.ANY),
                  pl.BlockSpec((tc, TOP_K), lambda i: (i, 0)),
                  pl.BlockSpec((tc, D_MODEL), lambda i: (i, 0))],
        out_specs=pl.BlockSpec((tc, D_MODEL), lambda i: (i, 0)),
        out_shape=jax.ShapeDtypeStruct((n, D_MODEL), F32),
        scratch_shapes=[pltpu.VMEM((TOP_K, tc, D_MODEL), F32), pltpu.SemaphoreType.DMA(())],
        compiler_params=_cparams(("arbitrary",)),
        name="moe_combine",
    )(dest.reshape(n // tc, 1, tc * TOP_K), y, gates, h)


def _moe(v, route, h, w_gate, w_up, w_down):
    n = h.shape[0]
    nk = n * TOP_K
    tm = TM_MOE
    e_flat = route[:TOP_K].T.astype(jnp.int32).reshape(nk)
    gates = route[TOP_K:2 * TOP_K].T
    onehot = (e_flat[:, None] == jnp.arange(N_EXPERTS)[None, :]).astype(jnp.int32)
    csum = jnp.cumsum(onehot, axis=0)
    counts = csum[-1]
    rank = jnp.sum(jnp.where(onehot > 0, csum, 0), axis=1) - 1
    padded = (counts + tm - 1) // tm * tm
    pend = jnp.cumsum(padded)
    pstart = pend - padded
    dest = (jnp.sum(jnp.where(onehot > 0, pstart[None, :], 0), axis=1) + rank).astype(jnp.int32)
    n_tiles = nk // tm + N_EXPERTS
    tile_start = jnp.arange(n_tiles) * tm
    tile_expert = jnp.minimum(jnp.sum(tile_start[:, None] >= pend[None, :], axis=1), N_EXPERTS - 1)
    tile_valid = (tile_start < pend[-1]).astype(jnp.int32)
    xs = _dispatch(v, dest, n_tiles * tm)
    y = _grouped_swiglu(xs, tile_expert.astype(jnp.int32), tile_valid, w_gate, w_up, w_down)
    return _combine(y, dest, gates, h)


def kernel(x, rel_bias, attn_norm, w_in, nsa_q_norm, nsa_k_norm, cmp_pos, cmp_w1, cmp_b1, cmp_w2,
           dil_q_norm, dil_k_norm, out_norm, w_out, ffn_norm, ffn_w_gate, ffn_w_up, ffn_w_down,
           router_w, exp_w_gate, exp_w_up, exp_w_down):
    batch, seq, _ = x.shape
    depth = attn_norm.shape[0]
    n = batch * seq
    assert seq % (DIL_PAIRS[-1][1] * DIL_ROWS) == 0 and n % DISPATCH_TOKENS == 0
    assert all(w == d * DIL_BAND for w, d in DIL_PAIRS)

    tbl = rel_bias.astype(F32).T
    tbl = tbl * LOG2E
    tbl_a, tbl_b = tbl[:N_HEADS_A], tbl[N_HEADS_A:]
    max_off = seq // TOEP - 1
    far_sel = min(max_off, -(-(FAR_DIST + TOEP - 1) // TOEP))
    far_win = min(max_off, -(-(WIN_A + TOEP - 1) // TOEP))
    bias_c = _bias_table(tbl_a, seq // TQ_NSA, TQ_NSA, seq // CMP_STRIDE,
                         functools.partial(_cmp_dist, n_cmp=(seq - CMP_LEN) // CMP_STRIDE + 1))
    bias_s = _bias_table(tbl_a, far_sel + 2, TOEP, TOEP, _sel_dist)
    bias_w = _bias_table(tbl_a, far_win + 2, TOEP, TOEP, _win_dist)
    bias_d = [_bias_table(tbl_b, 1, DIL_BAND, 2 * DIL_BAND, functools.partial(_dil_dist, dil=d))
              for _, d in DIL_PAIRS]

    h = x.reshape(n, D_MODEL)
    for layer in range(depth):
        qa, cva, ksw, ga, qkv_dilated = _in_projection(
            h, seq, attn_norm[layer], w_in[layer], nsa_q_norm[layer], nsa_k_norm[layer],
            dil_q_norm[layer], dil_k_norm[layer])
        kc, vc = _compress(cva, cmp_pos[layer], cmp_w1[layer], cmp_b1[layer], cmp_w2[layer],
                           nsa_k_norm[layer, 0])
        o_a = _nsa_attention(qa, kc, vc, ksw, ga, bias_c, bias_s, bias_w, batch, seq)
        dil_outs = [_dilated_attention(*qkv_dilated[i], bias_d[i], d) for i, (_, d) in enumerate(DIL_PAIRS)]
        moe_layer = layer % 2 == 1
        outs = _out_projection(o_a, dil_outs, h, out_norm[layer], w_out[layer], ffn_norm[layer],
                               router_w[layer // 2] if moe_layer else None)
        if moe_layer:
            h1, v, route = outs
            h = _moe(v, route, h1, exp_w_gate[layer // 2], exp_w_up[layer // 2], exp_w_down[layer // 2])
        else:
            h1, v = outs
            h = _dense_ffn(v, h1, ffn_w_gate[layer // 2], ffn_w_up[layer // 2], ffn_w_down[layer // 2])
    return h.reshape(batch, seq, D_MODEL)
```

```python
import functools
import math

import jax
import jax.numpy as jnp
import numpy as np
from jax import lax
from jax.experimental import pallas as pl
from jax.experimental.pallas import tpu as pltpu

F32 = jnp.float32
BF16 = jnp.bfloat16

D_MODEL = 1024
HEAD_DIM = 64
N_HEADS_A = 8
N_KV_A = 2
HPG_A = N_HEADS_A // N_KV_A
N_HEADS_B = 8
WIDTH_A = N_HEADS_A * HEAD_DIM
WIDTH_B = N_HEADS_B * HEAD_DIM
KV_A = N_KV_A * HEAD_DIM
CMP_LEN = 32
CMP_STRIDE = 16
CMP_HIDDEN = 256
SEL_BLOCK = 64
SEL_TOPK = 16
WIN_A = 512
DIL_PAIRS = ((128, 1), (512, 4), (2048, 16))
N_BUCKETS = 32
MAX_DISTANCE = 2048
N_EXPERTS = 8
TOP_K = 2
RMS_EPS = 1e-6
NEG = -1e30
SCALE = HEAD_DIM ** -0.5
LOG2E = math.log2(math.e)

LANES = 128
MXU_DIM = 256
VMEM_LIMIT = 56 * 1024 * 1024

TM_PROJ = 512
TQ_NSA = 256
TK_NSA = 512
TOEP = 128
DIL_BAND = 128
DIL_ROWS = 512
TM_FFN = 512
TF_FFN = 1408
TM_MOE = 1024
TF_MOE = 896
DISPATCH_TOKENS = 512
COMBINE_TOKENS = 256
DMA_ISSUE_UNROLL = 8

COL_QA = 0
COL_CVA = 512
COL_KSW = 768
COL_QB = 1280
COL_KB = 1792
COL_VB = 2304
COL_GA = 2816
IN_COLS = 2944
PAIRED_HEADS_A = np.array([h + g * HPG_A for h in range(HPG_A) for g in range(N_KV_A)])


def _cparams(sem, vmem=VMEM_LIMIT):
    return pltpu.CompilerParams(dimension_semantics=sem, vmem_limit_bytes=vmem)


def _dot(a, b):
    return jnp.dot(a, b, preferred_element_type=F32)


def _dot_nt(a, b):
    return lax.dot_general(a, b, (((1,), (1,)), ((), ())), preferred_element_type=F32)


def _split_bf16(x):
    hi = x.astype(BF16)
    lo = (x - hi.astype(F32)).astype(BF16)
    return hi, lo


def _bucket_thresholds():
    d = np.arange(0, 4 * MAX_DISTANCE, dtype=np.int64)
    max_exact = N_BUCKETS // 2
    scaled = np.log(np.maximum(d, 1).astype(np.float32) / np.float32(max_exact)) / np.float32(
        math.log(MAX_DISTANCE / max_exact))
    large = np.minimum(max_exact + (scaled.astype(np.float32) * (N_BUCKETS - max_exact)).astype(np.int32),
                       N_BUCKETS - 1)
    bucket = np.where(d < max_exact, d, large)
    assert np.all(np.diff(bucket) >= 0)
    return [int(np.argmax(bucket >= b)) for b in range(N_BUCKETS)]


_THR = _bucket_thresholds()
FAR_DIST = _THR[N_BUCKETS - 1]


def _bias_table_kernel(tbl_ref, out_ref, *, n_heads, rows, cols, dist_valid):
    i = pl.program_id(0)
    a = lax.broadcasted_iota(jnp.int32, (rows, cols), 0)
    c = lax.broadcasted_iota(jnp.int32, (rows, cols), 1)
    d, valid = dist_valid(i, a, c)
    for h in range(n_heads):
        acc = jnp.full((rows, cols), tbl_ref[h, 0], F32)
        for b in range(1, N_BUCKETS):
            acc = jnp.where(d >= _THR[b], tbl_ref[h, b], acc)
        out_ref[h, 0] = jnp.where(valid, acc, NEG)


def _bias_table(tbl, n_tiles, rows, cols, dist_valid):
    n_heads = tbl.shape[0]
    return pl.pallas_call(
        functools.partial(_bias_table_kernel, n_heads=n_heads, rows=rows, cols=cols, dist_valid=dist_valid),
        grid=(n_tiles,),
        in_specs=[pl.BlockSpec(memory_space=pltpu.SMEM)],
        out_specs=pl.BlockSpec((n_heads, 1, rows, cols), lambda i: (0, i, 0, 0)),
        out_shape=jax.ShapeDtypeStruct((n_heads, n_tiles, rows, cols), F32),
        compiler_params=_cparams(("arbitrary",)),
        name="bias_table",
    )(tbl)


def _cmp_dist(i, a, c, *, n_cmp):
    d = i * TQ_NSA + a - (c * CMP_STRIDE + CMP_LEN - 1)
    return d, (d >= 0) & (c < n_cmp)


def _sel_dist(i, a, c):
    d = (i - 1) * TOEP + a - c
    return d, d >= 0


def _win_dist(i, a, c):
    d = (i - 1) * TOEP + a - c
    return d, (d >= 0) & (d < WIN_A)


def _dil_dist(i, a, c, *, dil):
    n = DIL_BAND + a - c
    return n * dil, (n >= 0) & (n <= DIL_BAND)


def _inproj_kernel(h_ref, gn_ref, w_ref, gain_ref, bd_ref,
                   qa_ref, cva_ref, ksw_ref, ga_ref, *rest):
    dil_refs, stage_ref = rest[:-1], rest[-1]
    tm = h_ref.shape[0]

    def emit_dilated(y, which):
        dil_refs[which][0, 0] = y.astype(BF16)
        n_chunks = y.shape[1] // LANES
        for c in range(n_chunks):
            stage_ref[c] = y[:, c * LANES:(c + 1) * LANES]
        for d, (_, r) in enumerate(DIL_PAIRS):
            if r == 1:
                continue
            ref = dil_refs[3 * d + which]
            for rho in range(r):
                for c in range(n_chunks):
                    ref[0, rho, :, c * LANES:(c + 1) * LANES] = stage_ref[
                        c, pl.ds(rho, tm // r, stride=r), :].astype(BF16)

    x = h_ref[...]
    u = (x * lax.rsqrt(jnp.mean(x * x, axis=-1, keepdims=True) + RMS_EPS) * gn_ref[...]).astype(BF16)

    def proj(c0, width):
        return _dot(u, w_ref[:, c0:c0 + width])

    def headnorm(acc, c0):
        outs = []
        for j in range(acc.shape[1] // MXU_DIM):
            a = acc[:, j * MXU_DIM:(j + 1) * MXU_DIM]
            sq_hi, sq_lo = _split_bf16(a * a)
            ms = _dot(sq_hi, bd_ref[...]) + _dot(sq_lo, bd_ref[...])
            g = gain_ref[:, c0 + j * MXU_DIM:c0 + (j + 1) * MXU_DIM]
            outs.append(a * lax.rsqrt(ms + RMS_EPS) * g)
        return outs[0] if len(outs) == 1 else jnp.concatenate(outs, axis=1)

    qa_ref[...] = headnorm(proj(COL_QA, WIDTH_A), COL_QA).astype(BF16)
    cva = proj(COL_CVA, 2 * KV_A)
    for c in range(2 * KV_A // LANES):
        stage_ref[c] = cva[:, c * LANES:(c + 1) * LANES]
    for l in range(CMP_STRIDE):
        for c in range(2 * KV_A // LANES):
            cva_ref[0, :, l * 2 * KV_A + c * LANES:l * 2 * KV_A + (c + 1) * LANES] = stage_ref[
                c, pl.ds(l, tm // CMP_STRIDE, stride=CMP_STRIDE), :]
    ksw = proj(COL_KSW, 4 * KV_A)
    ksw_ref[:, :2 * KV_A] = headnorm(ksw[:, :2 * KV_A], COL_KSW).astype(BF16)
    ksw_ref[:, 2 * KV_A:] = ksw[:, 2 * KV_A:].astype(BF16)
    emit_dilated(headnorm(proj(COL_QB, WIDTH_B), COL_QB), 0)
    emit_dilated(headnorm(proj(COL_KB, WIDTH_B), COL_KB), 1)
    emit_dilated(proj(COL_VB, WIDTH_B), 2)
    ga_ref[...] = jax.nn.sigmoid(proj(COL_GA, LANES))


def _in_projection(h, seq, attn_norm, w_in, nsa_q_norm, nsa_k_norm, dil_q_norm, dil_k_norm):
    n = h.shape[0]
    o = np.cumsum((0, WIDTH_A, KV_A, KV_A, KV_A, KV_A, KV_A, KV_A, N_HEADS_A * 3, WIDTH_B, WIDTH_B, WIDTH_B))
    seg = [w_in[:, o[i]:o[i + 1]] for i in range(11)]
    qa, kc, vc, ks, vs, kw, vw, ga, qb, kb, vb = seg
    qa = qa.reshape(D_MODEL, N_HEADS_A, HEAD_DIM)[:, PAIRED_HEADS_A, :].reshape(D_MODEL, WIDTH_A)
    pad = jnp.zeros((D_MODEL, IN_COLS - COL_GA - N_HEADS_A * 3), w_in.dtype)
    w = jnp.concatenate([qa, kc, vc, ks, kw, vs, vw, qb, kb, vb, ga, pad], axis=1).astype(BF16)
    ones = jnp.ones((IN_COLS,), F32)
    gain = ones
    gain = gain.at[COL_QA:COL_QA + WIDTH_A].set(jnp.tile(nsa_q_norm, N_HEADS_A) * (SCALE * LOG2E))
    gain = gain.at[COL_KSW:COL_KSW + KV_A].set(jnp.tile(nsa_k_norm[1], N_KV_A))
    gain = gain.at[COL_KSW + KV_A:COL_KSW + 2 * KV_A].set(jnp.tile(nsa_k_norm[2], N_KV_A))
    gain = gain.at[COL_QB:COL_QB + WIDTH_B].set(jnp.tile(dil_q_norm, N_HEADS_B) * (SCALE * LOG2E))
    gain = gain.at[COL_KB:COL_KB + WIDTH_B].set(jnp.tile(dil_k_norm, N_HEADS_B))
    blk = np.arange(MXU_DIM) // HEAD_DIM
    bd = jnp.asarray((blk[:, None] == blk[None, :]).astype(np.float32) / HEAD_DIM, BF16)

    tm = TM_PROJ
    tiles_per_seq = seq // tm
    row = lambda width: pl.BlockSpec((tm, width), lambda i: (i, 0))
    full = lambda a: pl.BlockSpec(a.shape, lambda i: (0,) * a.ndim)
    gn = attn_norm.reshape(1, D_MODEL)
    gain = gain.reshape(1, IN_COLS)
    dil_specs, dil_shapes = [], []
    for _, r in DIL_PAIRS:
        spec = pl.BlockSpec((1, r, tm // r, WIDTH_B), lambda i: (i // tiles_per_seq, 0, i % tiles_per_seq, 0))
        dil_specs += [spec] * 3
        dil_shapes += [jax.ShapeDtypeStruct((n // seq, r, seq // r, WIDTH_B), BF16)] * 3
    outs = pl.pallas_call(
        _inproj_kernel,
        grid=(n // tm,),
        in_specs=[row(D_MODEL), full(gn), full(w), full(gain), full(bd)],
        out_specs=[row(WIDTH_A),
                   pl.BlockSpec((1, tm // CMP_STRIDE, CMP_STRIDE * 2 * KV_A),
                                lambda i: (i // tiles_per_seq, i % tiles_per_seq, 0)),
                   row(4 * KV_A), row(LANES)] + dil_specs,
        out_shape=[jax.ShapeDtypeStruct((n, WIDTH_A), BF16),
                   jax.ShapeDtypeStruct((n // seq, seq // CMP_STRIDE, CMP_STRIDE * 2 * KV_A), F32),
                   jax.ShapeDtypeStruct((n, 4 * KV_A), BF16), jax.ShapeDtypeStruct((n, LANES), F32)] + dil_shapes,
        scratch_shapes=[pltpu.VMEM((WIDTH_B // LANES, tm, LANES), F32)],
        compiler_params=_cparams(("parallel",)),
        name="in_projection",
    )(h, gn, w, gain, bd)
    qa, cva, ksw, ga = outs[:4]
    qkv_dilated = [outs[4 + 3 * d:7 + 3 * d] for d in range(len(DIL_PAIRS))]
    return qa, cva, ksw, ga, qkv_dilated


def _gelu_tanh(x):
    return 0.5 * x * (1.0 + jnp.tanh(math.sqrt(2.0 / math.pi) * (x + 0.044715 * (x * x * x))))


def _compress_kernel(x_ref, pos_ref, w1_ref, b1_ref, w2_ref, kg_ref, kc_ref, vc_ref):
    rows = x_ref.shape[1]
    half = CMP_LEN // 2
    for which, out_ref in ((0, kc_ref), (1, vc_ref)):
        top = jnp.zeros((rows, 2 * CMP_HIDDEN), F32)
        bot = jnp.zeros((rows, 2 * CMP_HIDDEN), F32)
        for l in range(half):
            c0 = l * 2 * KV_A + which * KV_A
            a = x_ref[0, :, c0:c0 + KV_A]
            top += _dot((a + pos_ref[which, l:l + 1, :]).astype(BF16), w1_ref[which, l])
            bot += _dot((a + pos_ref[which, half + l:half + l + 1, :]).astype(BF16), w1_ref[which, half + l])
        hid = top + pltpu.roll(bot, rows - 1, axis=0) + b1_ref[which]
        y = _dot(_gelu_tanh(hid).astype(BF16), w2_ref[which])
        if which == 0:
            parts = []
            for g in range(N_KV_A):
                yg = y[:, g * HEAD_DIM:(g + 1) * HEAD_DIM]
                parts.append(yg * lax.rsqrt(jnp.mean(yg * yg, axis=-1, keepdims=True) + RMS_EPS))
            y = jnp.concatenate(parts, axis=1) * kg_ref[...]
        out_ref[0] = y.astype(BF16)


def _compress(x, cmp_pos, cmp_w1, cmp_b1, cmp_w2, k_norm0):
    batch, rows, _ = x.shape
    pos = jnp.tile(cmp_pos, (1, 1, N_KV_A))
    w1 = cmp_w1.reshape(2, CMP_LEN, HEAD_DIM, CMP_HIDDEN).astype(BF16)
    z1 = jnp.zeros_like(w1)
    w1 = jnp.concatenate([jnp.concatenate([w1, z1], axis=3), jnp.concatenate([z1, w1], axis=3)], axis=2)
    b1 = jnp.tile(cmp_b1, (1, N_KV_A)).reshape(2, 1, 2 * CMP_HIDDEN)
    w2 = cmp_w2.astype(BF16)
    z2 = jnp.zeros_like(w2)
    w2 = jnp.concatenate([jnp.concatenate([w2, z2], axis=2), jnp.concatenate([z2, w2], axis=2)], axis=1)
    kg = jnp.tile(k_norm0, N_KV_A).reshape(1, KV_A)
    full = lambda a: pl.BlockSpec(a.shape, lambda b: (0,) * a.ndim)
    out = pl.BlockSpec((1, rows, KV_A), lambda b: (b, 0, 0))
    return pl.pallas_call(
        _compress_kernel,
        grid=(batch,),
        in_specs=[pl.BlockSpec((1, rows, x.shape[2]), lambda b: (b, 0, 0)),
                  full(pos), full(w1), full(b1), full(w2), full(kg)],
        out_specs=[out, out],
        out_shape=[jax.ShapeDtypeStruct((batch, rows, KV_A), BF16)] * 2,
        compiler_params=_cparams(("parallel",)),
        name="nsa_compress",
    )(x, pos, w1, b1, w2, kg)


def _toeplitz_bias(tbl_ref, g, base, n_a, n_c):
    far = tbl_ref.shape[1] - 2
    rows = []
    for a in range(n_a):
        tiles = [tbl_ref[g * HPG_A:(g + 1) * HPG_A, jnp.clip(base + a - c, -1, far) + 1] for c in range(n_c)]
        rows.append(jnp.concatenate(tiles, axis=2))
    return jnp.concatenate(rows, axis=1)


def _nsa_kernel(q_ref, kc_ref, vc_ref, ksw_ref, ga_ref, bc_ref, bs_ref, bw_ref, ex_ref, gp_ref, o_ref,
                imp_ref, *, seq):
    tq, tk = TQ_NSA, TK_NSA
    qi = pl.program_id(1)
    rows_c = seq // CMP_STRIDE
    n_sb = seq // SEL_BLOCK
    k_sel = min(SEL_TOPK, n_sb)
    rows = HPG_A * tq
    groups = range(N_KV_A)

    lane = lax.broadcasted_iota(jnp.int32, (1, LANES), 1)
    low_half = lane < HEAD_DIM
    half_bf = [jnp.where(low_half, 1.0, 0.0).astype(BF16), jnp.where(low_half, 0.0, 1.0).astype(BF16)]
    qs = [jnp.concatenate([q_ref[:, c * LANES:(c + 1) * LANES] * half_bf[g] for c in range(HPG_A)], axis=0)
          for g in groups]

    def row_sums(p_bf):
        return _dot(p_bf, jnp.ones((p_bf.shape[1], LANES), BF16))

    jj = lax.broadcasted_iota(jnp.int32, (n_sb, rows_c), 0) * SEL_BLOCK
    nn = lax.broadcasted_iota(jnp.int32, (n_sb, rows_c), 1) * CMP_STRIDE
    ov = jnp.maximum(jnp.minimum(nn + CMP_LEN, jj + SEL_BLOCK) - jnp.maximum(nn, jj), 0)
    ov_t = (ov.astype(F32) * (1.0 / CMP_LEN)).astype(BF16)
    blk = lax.broadcasted_iota(jnp.int32, (n_sb, tq), 0)
    tpos = qi * tq + lax.broadcasted_iota(jnp.int32, (n_sb, tq), 1)
    cur = tpos // SEL_BLOCK
    forced = (blk == 0) | (blk == cur) | (blk == cur - 1)
    future = blk * SEL_BLOCK > tpos

    o_c, sel_bf = [], []
    for g in groups:
        s = _dot_nt(qs[g], kc_ref[0]).reshape(HPG_A, tq, rows_c) + bc_ref[g * HPG_A:(g + 1) * HPG_A, 0]
        m = jnp.max(s, axis=-1, keepdims=True)
        e = jnp.where(s > 0.5 * NEG, jnp.exp2(s - m), 0.0)
        e_bf = e.reshape(rows, rows_c).astype(BF16)
        inv = 1.0 / jnp.maximum(row_sums(e_bf), 1e-30)
        o_c.append(_dot(e_bf, vc_ref[0]) * inv)
        inv4 = inv.reshape(HPG_A, tq, LANES)
        p = e * jnp.concatenate([inv4] * (rows_c // LANES), axis=2)
        p_sum = p[0] + p[1] + p[2] + p[3]
        p_hi, p_lo = _split_bf16(p_sum)
        p_lo2 = (p_sum - p_hi.astype(F32) - p_lo.astype(F32)).astype(BF16)
        imp = _dot_nt(ov_t, p_hi) + _dot_nt(ov_t, p_lo) + _dot_nt(ov_t, p_lo2)
        imp = jnp.where(forced, 1e6, jnp.where(future, -1e6, imp))
        imp_ref[...] = imp

        def count_beaten(i0, cnt):
            for u in range(tq // SEL_BLOCK):
                i = i0 * (tq // SEL_BLOCK) + u
                ri = jnp.broadcast_to(imp_ref[pl.ds(i, 1), :], (n_sb, tq))
                later = jnp.where(blk > i, 1.0, 0.0)
                cnt = cnt + jnp.where(ri > imp, 1.0, jnp.where(ri == imp, later, 0.0))
            return cnt

        cnt = lax.fori_loop(0, qi + 1, count_beaten, jnp.zeros((n_sb, tq), F32))
        sel_bf.append(jnp.where(cnt < k_sel, 1.0, 0.0).T.astype(BF16))

    def with_ones(v_pair, g):
        return v_pair * half_bf[g] + half_bf[1 - g]

    def normalised(acc):
        return acc / pltpu.roll(acc, HEAD_DIM, axis=1)

    def sel_body(kj, carry):
        r0 = pl.multiple_of(kj * tk, tk)
        k = ksw_ref[pl.ds(r0, tk), 0:KV_A]
        v = ksw_ref[pl.ds(r0, tk), 2 * KV_A:3 * KV_A]
        out = []
        for g in groups:
            m, acc = carry[g]
            s = _dot_nt(qs[g], k).reshape(HPG_A, tq, tk)
            s = s + _toeplitz_bias(bs_ref, g, (tq // TOEP) * qi - (tk // TOEP) * kj, tq // TOEP, tk // TOEP)
            madd = (_dot(sel_bf[g], ex_ref[kj]) - 1.0) * (-NEG)
            s = (s + madd[None]).reshape(rows, tk)
            m_new = jnp.maximum(m, jnp.max(s, axis=-1, keepdims=True))
            p = jnp.exp2(s - m_new).astype(BF16)
            out.append((m_new, jnp.exp2(m - m_new) * acc + _dot(p, with_ones(v, g))))
        return tuple(out)

    init = tuple((jnp.full((rows, 1), NEG, F32), jnp.zeros((rows, LANES), F32)) for _ in groups)
    n_tiles = (qi * tq + tq + tk - 1) // tk
    sel_out = lax.fori_loop(0, n_tiles, sel_body, init)
    o_s = [normalised(acc) for _, acc in sel_out]

    n_wk = WIN_A + tq
    start = pl.multiple_of(jnp.maximum(qi * tq - WIN_A, 0), tq)
    kw = ksw_ref[pl.ds(start, n_wk), KV_A:2 * KV_A]
    vw = ksw_ref[pl.ds(start, n_wk), 3 * KV_A:4 * KV_A]
    o_w = []
    for g in groups:
        s = _dot_nt(qs[g], kw).reshape(HPG_A, tq, n_wk)
        s = s + _toeplitz_bias(bw_ref, g, (qi * tq - start) // TOEP, tq // TOEP, n_wk // TOEP)
        s = s.reshape(rows, n_wk)
        p = jnp.exp2(s - jnp.max(s, axis=-1, keepdims=True)).astype(BF16)
        o_w.append(normalised(_dot(p, with_ones(vw, g))))

    g_hi, g_lo = _split_bf16(ga_ref[...])
    gates = _dot(g_hi, gp_ref[...]) + _dot(g_lo, gp_ref[...])
    for c in range(HPG_A):
        rs = slice(c * tq, (c + 1) * tq)
        out = jnp.zeros((tq, LANES), F32)
        for j, o in enumerate((o_c, o_s, o_w)):
            gate = gates[:, j * WIDTH_A + c * LANES:j * WIDTH_A + (c + 1) * LANES]
            out = out + gate * jnp.where(low_half, o[0][rs], o[1][rs])
        o_ref[:, c * LANES:(c + 1) * LANES] = out


def _nsa_attention(qa, kc, vc, ksw, ga, bias_c, bias_s, bias_w, batch, seq):
    tq, tk = TQ_NSA, TK_NSA
    nq = seq // tq
    n = batch * seq
    rows_c = seq // CMP_STRIDE
    n_sb = seq // SEL_BLOCK
    assert rows_c % LANES == 0 and seq % tk == 0 and seq >= WIN_A + tq and WIN_A % tq == 0
    key_blk = (np.arange(seq) // SEL_BLOCK).reshape(seq // tk, 1, tk)
    expand = jnp.asarray((key_blk == np.arange(n_sb).reshape(1, n_sb, 1)).astype(np.float32), BF16)
    gp = np.zeros((LANES, 3 * WIDTH_A), np.float32)
    for pos, head in enumerate(PAIRED_HEADS_A):
        for j in range(3):
            gp[3 * head + j, j * WIDTH_A + pos * HEAD_DIM:j * WIDTH_A + (pos + 1) * HEAD_DIM] = 1.0
    gp = jnp.asarray(gp, BF16)
    resident = lambda a: pl.BlockSpec(a.shape, lambda b, i: (0,) * a.ndim, pipeline_mode=pl.Buffered(1))
    return pl.pallas_call(
        functools.partial(_nsa_kernel, seq=seq),
        grid=(batch, nq),
        in_specs=[
            pl.BlockSpec((tq, WIDTH_A), lambda b, i: (b * nq + i, 0)),
            pl.BlockSpec((1, rows_c, KV_A), lambda b, i: (b, 0, 0)),
            pl.BlockSpec((1, rows_c, KV_A), lambda b, i: (b, 0, 0)),
            pl.BlockSpec((seq, 4 * KV_A), lambda b, i: (b, 0)),
            pl.BlockSpec((tq, LANES), lambda b, i: (b * nq + i, 0)),
            pl.BlockSpec((N_HEADS_A, 1, tq, rows_c), lambda b, i: (0, i, 0, 0)),
            resident(bias_s), resident(bias_w), resident(expand), resident(gp),
        ],
        out_specs=pl.BlockSpec((tq, WIDTH_A), lambda b, i: (b * nq + i, 0)),
        out_shape=jax.ShapeDtypeStruct((n, WIDTH_A), F32),
        scratch_shapes=[pltpu.VMEM((n_sb, tq), F32)],
        compiler_params=_cparams(("parallel", "arbitrary")),
        name="nsa_attention",
    )(qa, kc, vc, ksw, ga, bias_c, bias_s, bias_w, expand, gp)


def _dilated_kernel(q_ref, kp_ref, kc_ref, vp_ref, vc_ref, bias_ref, o_ref, lse_ref):
    band = DIL_BAND
    first = pl.program_id(2) == 0
    prev_mask = jnp.where(first, NEG, 0.0)
    lane = lax.broadcasted_iota(jnp.int32, (1, LANES), 1)
    low_half = lane < HEAD_DIM
    half_bf = [jnp.where(low_half, 1.0, 0.0).astype(BF16), jnp.where(low_half, 0.0, 1.0).astype(BF16)]
    den_sel = [jnp.broadcast_to(hm, (2 * band, LANES)) for hm in half_bf]
    key_col = lax.broadcasted_iota(jnp.int32, (1, 2 * band), 1)
    prev_cols = jnp.where(key_col < band, prev_mask, 0.0)
    for sub in range(q_ref.shape[0] // band):
        r_cur = slice(sub * band, (sub + 1) * band)
        for c in range(WIDTH_B // LANES):
            cols = slice(c * LANES, (c + 1) * LANES)
            q = q_ref[r_cur, cols]
            if sub == 0:
                k_cat = jnp.concatenate([kp_ref[:, cols], kc_ref[r_cur, cols]], axis=0)
                v_cat = jnp.concatenate([vp_ref[:, cols], vc_ref[r_cur, cols]], axis=0)
            else:
                k_cat = kc_ref[(sub - 1) * band:(sub + 1) * band, cols]
                v_cat = vc_ref[(sub - 1) * band:(sub + 1) * band, cols]
            acc = jnp.zeros((band, 2 * LANES), F32)
            ms = []
            for j in range(2):
                h = 2 * c + j
                s = _dot_nt(q * half_bf[j], k_cat) + bias_ref[h, 0]
                if sub == 0:
                    s = s + prev_cols
                m = jnp.max(s, axis=-1, keepdims=True)
                e = jnp.exp2(s - m).astype(BF16)
                acc = acc + _dot(e, jnp.concatenate([v_cat * half_bf[j], den_sel[j]], axis=1))
                ms.append(m)
            den = acc[:, LANES:]
            o_ref[r_cur, cols] = acc[:, :LANES] / den
            lse_ref[r_cur, cols] = jnp.where(low_half, ms[0], ms[1]) + jnp.log2(den)


def _dilated_attention(q, k, v, bias, dil):
    band = DIL_BAND
    batch, _, length, _ = q.shape
    rows = min(DIL_ROWS, length)
    sub = rows // band
    cur = pl.BlockSpec((None, None, rows, WIDTH_B), lambda b, r, n: (b, r, n, 0))
    prev = pl.BlockSpec((None, None, band, WIDTH_B), lambda b, r, n: (b, r, jnp.maximum(n * sub - 1, 0), 0))
    return pl.pallas_call(
        _dilated_kernel,
        grid=(batch, dil, length // rows),
        in_specs=[cur, prev, cur, prev, cur,
                  pl.BlockSpec(bias.shape, lambda b, r, n: (0, 0, 0, 0))],
        out_specs=[cur, cur],
        out_shape=[jax.ShapeDtypeStruct((batch, dil, length, WIDTH_B), F32)] * 2,
        compiler_params=_cparams(("parallel", "parallel", "arbitrary")),
        name=f"dilated_attention_d{dil}",
    )(q, k, k, v, v, bias)


def _outproj_kernel(*refs, with_router):
    if with_router:
        (oa_ref, o1_ref, o2_ref, o3_ref, l1_ref, l2_ref, l3_ref, h_ref, gn_ref, w_ref, fg_ref, rw_ref,
         h_out_ref, v_ref, route_ref, stage_ref) = refs
    else:
        (oa_ref, o1_ref, o2_ref, o3_ref, l1_ref, l2_ref, l3_ref, h_ref, gn_ref, w_ref, fg_ref,
         h_out_ref, v_ref, stage_ref) = refs
    tm = h_ref.shape[0]

    def token_order(ref, slot):
        r = ref.shape[1]
        if r == 1:
            return ref[0, 0]
        n_chunks = ref.shape[3] // LANES
        for rho in range(r):
            for c in range(n_chunks):
                stage_ref[slot, c, pl.ds(rho, tm // r, stride=r), :] = ref[0, rho, :, c * LANES:(c + 1) * LANES]
        return jnp.concatenate([stage_ref[slot, c] for c in range(n_chunks)], axis=1)

    l1, l2, l3 = token_order(l1_ref, 0), token_order(l2_ref, 0), token_order(l3_ref, 1)
    m = jnp.maximum(jnp.maximum(l1, l2), l3)
    e1, e2, e3 = jnp.exp2(l1 - m), jnp.exp2(l2 - m), jnp.exp2(l3 - m)
    o1, o2, o3 = token_order(o1_ref, 0), token_order(o2_ref, 2), token_order(o3_ref, 3)
    ob = (e1 * o1 + e2 * o2 + e3 * o3) / (e1 + e2 + e3)

    def norm(x, g):
        return x * lax.rsqrt(jnp.mean(x * x, axis=-1, keepdims=True) + RMS_EPS) * g

    o = jnp.concatenate([norm(oa_ref[...], gn_ref[:, :WIDTH_A]), norm(ob, gn_ref[:, WIDTH_A:])], axis=1)
    h = h_ref[...] + _dot(o.astype(BF16), w_ref[...])
    h_out_ref[...] = h
    v = norm(h, fg_ref[...])
    v_ref[...] = v.astype(v_ref.dtype)
    if with_router:
        v_hi, v_lo = _split_bf16(v)
        w_hi, w_lo = _split_bf16(rw_ref[...])
        logits = _dot(v_hi, w_hi) + _dot(v_lo, w_hi) + _dot(v_hi, w_lo)
        lg = logits.T[:N_EXPERTS]
        row = lax.broadcasted_iota(jnp.int32, lg.shape, 0).astype(F32)
        m1 = jnp.max(lg, axis=0, keepdims=True)
        i1 = jnp.min(jnp.where(lg == m1, row, float(N_EXPERTS)), axis=0, keepdims=True)
        lg2 = jnp.where(row == i1, -jnp.inf, lg)
        m2 = jnp.max(lg2, axis=0, keepdims=True)
        i2 = jnp.min(jnp.where(lg2 == m2, row, float(N_EXPERTS)), axis=0, keepdims=True)
        e = jnp.exp(m2 - m1)
        g1 = 1.0 / (1.0 + e)
        g2 = e / (1.0 + e)
        route_ref[...] = jnp.where(row == 0, i1, jnp.where(
            row == 1, i2, jnp.where(row == 2, g1, jnp.where(row == 3, g2, 0.0))))


def _out_projection(o_a, dil_outs, h, out_norm, w_out, ffn_norm, router_w):
    n = h.shape[0]
    tm = TM_PROJ
    with_router = router_w is not None
    row = lambda width: pl.BlockSpec((tm, width), lambda i: (i, 0))
    full = lambda a: pl.BlockSpec(a.shape, lambda i: (0,) * a.ndim)
    (o1, l1), (o2, l2), (o3, l3) = dil_outs
    gn_a = out_norm[:WIDTH_A].reshape(N_HEADS_A, HEAD_DIM)[PAIRED_HEADS_A].reshape(WIDTH_A)
    gn = jnp.concatenate([gn_a, out_norm[WIDTH_A:]]).reshape(1, -1)
    fg = ffn_norm.reshape(1, -1)
    w_a = w_out[:WIDTH_A].reshape(N_HEADS_A, HEAD_DIM, D_MODEL)[PAIRED_HEADS_A].reshape(WIDTH_A, D_MODEL)
    w = jnp.concatenate([w_a, w_out[WIDTH_A:]], axis=0).astype(BF16)
    tiles_per_seq = o1.shape[2] // tm
    grouped = lambda a: pl.BlockSpec((1, a.shape[1], tm // a.shape[1], WIDTH_B),
                                     lambda i: (i // tiles_per_seq, 0, i % tiles_per_seq, 0))
    args = [o_a, o1, o2, o3, l1, l2, l3, h, gn, w, fg]
    in_specs = ([row(WIDTH_A)] + [grouped(a) for a in args[1:7]]
                + [row(D_MODEL), full(gn), full(w), full(fg)])
    out_specs = [row(D_MODEL), row(D_MODEL)]
    out_shape = [jax.ShapeDtypeStruct((n, D_MODEL), F32),
                 jax.ShapeDtypeStruct((n, D_MODEL), F32 if with_router else BF16)]
    if with_router:
        rw = jnp.pad(router_w, ((0, 0), (0, LANES - N_EXPERTS)))
        args.append(rw)
        in_specs.append(full(rw))
        out_specs.append(pl.BlockSpec((N_EXPERTS, tm), lambda i: (0, i)))
        out_shape.append(jax.ShapeDtypeStruct((N_EXPERTS, n), F32))
    return pl.pallas_call(
        functools.partial(_outproj_kernel, with_router=with_router),
        grid=(n // tm,),
        in_specs=in_specs,
        out_specs=out_specs,
        out_shape=out_shape,
        scratch_shapes=[pltpu.VMEM((4, WIDTH_B // LANES, tm, LANES), F32)],
        compiler_params=_cparams(("parallel",)),
        name="out_projection",
    )(*args)


def _ffn_kernel(v_ref, h_ref, wg_ref, wu_ref, wd_ref, o_ref, acc_ref):
    f = pl.program_id(1)

    @pl.when(f == 0)
    def _():
        acc_ref[...] = jnp.zeros_like(acc_ref)

    v = v_ref[...]
    gate = _dot(v, wg_ref[...])
    up = _dot(v, wu_ref[...])
    acc_ref[...] += _dot((jax.nn.silu(gate) * up).astype(BF16), wd_ref[...])

    @pl.when(f == pl.num_programs(1) - 1)
    def _():
        o_ref[...] = h_ref[...] + acc_ref[...]


def _dense_ffn(v, h, w_gate, w_up, w_down):
    n = h.shape[0]
    d_ff = w_gate.shape[1]
    tm, tf = TM_FFN, TF_FFN
    return pl.pallas_call(
        _ffn_kernel,
        grid=(n // tm, d_ff // tf),
        in_specs=[pl.BlockSpec((tm, D_MODEL), lambda i, f: (i, 0)),
                  pl.BlockSpec((tm, D_MODEL), lambda i, f: (i, 0)),
                  pl.BlockSpec((D_MODEL, tf), lambda i, f: (0, f)),
                  pl.BlockSpec((D_MODEL, tf), lambda i, f: (0, f)),
                  pl.BlockSpec((tf, D_MODEL), lambda i, f: (f, 0))],
        out_specs=pl.BlockSpec((tm, D_MODEL), lambda i, f: (i, 0)),
        out_shape=jax.ShapeDtypeStruct((n, D_MODEL), F32),
        scratch_shapes=[pltpu.VMEM((tm, D_MODEL), F32)],
        compiler_params=_cparams(("parallel", "arbitrary")),
        name="dense_ffn",
    )(v, h, w_gate.astype(BF16), w_up.astype(BF16), w_down.astype(BF16))


def _row_copy(src_ref, src_row, dst_ref, dst_row, sem):
    return pltpu.make_async_copy(src_ref.at[pl.ds(src_row, 1)], dst_ref.at[pl.ds(dst_row, 1)], sem)


def _dispatch_kernel(dest_ref, v_ref, xs_in_ref, xs_ref, sem):
    del xs_in_ref
    n_slots = dest_ref.shape[2]

    def issue(i, carry):
        for k in range(TOP_K):
            _row_copy(v_ref, i, xs_ref, dest_ref[0, 0, TOP_K * i + k], sem).start()
        return carry

    lax.fori_loop(0, n_slots // TOP_K, issue, 0, unroll=DMA_ISSUE_UNROLL)
    pltpu.make_async_copy(xs_ref.at[pl.ds(0, n_slots)], xs_ref.at[pl.ds(0, n_slots)], sem).wait()


def _dispatch(v, dest, n_rows):
    n = v.shape[0]
    tokens = DISPATCH_TOKENS
    slots = tokens * TOP_K
    return pl.pallas_call(
        _dispatch_kernel,
        grid=(n // tokens,),
        in_specs=[pl.BlockSpec((1, 1, slots), lambda i: (i, 0, 0), memory_space=pltpu.SMEM),
                  pl.BlockSpec((tokens, D_MODEL), lambda i: (i, 0)),
                  pl.BlockSpec(memory_space=pl.ANY)],
        out_specs=pl.BlockSpec(memory_space=pl.ANY),
        out_shape=jax.ShapeDtypeStruct((n_rows, D_MODEL), F32),
        scratch_shapes=[pltpu.SemaphoreType.DMA(())],
        input_output_aliases={2: 0},
        compiler_params=_cparams(("arbitrary",)),
        name="moe_dispatch",
    )(dest.reshape(n // tokens, 1, slots), v, jnp.zeros((n_rows, D_MODEL), F32))


def _gmm_kernel(te_ref, tv_ref, x_ref, wg_ref, wu_ref, wd_ref, y_ref, xb_ref, acc_ref):
    c = pl.program_id(0)
    f = pl.program_id(1)
    valid = tv_ref[c] > 0

    @pl.when(f == 0)
    def _():
        acc_ref[...] = jnp.zeros_like(acc_ref)
        xb_ref[...] = x_ref[...].astype(BF16)

    @pl.when(valid)
    def _():
        x = xb_ref[...]
        gate = _dot(x, wg_ref[0])
        up = _dot(x, wu_ref[0])
        acc_ref[...] += _dot((jax.nn.silu(gate) * up).astype(BF16), wd_ref[0])

    @pl.when(f == pl.num_programs(1) - 1)
    def _():
        y_ref[...] = acc_ref[...]


def _grouped_swiglu(xs, tile_expert, tile_valid, w_gate, w_up, w_down):
    n_rows = xs.shape[0]
    d_ff = w_gate.shape[2]
    tm, tf = TM_MOE, TF_MOE
    grid_spec = pltpu.PrefetchScalarGridSpec(
        num_scalar_prefetch=2,
        grid=(n_rows // tm, d_ff // tf),
        in_specs=[pl.BlockSpec((tm, D_MODEL), lambda c, f, te, tv: (c, 0)),
                  pl.BlockSpec((1, D_MODEL, tf), lambda c, f, te, tv: (te[c], 0, f)),
                  pl.BlockSpec((1, D_MODEL, tf), lambda c, f, te, tv: (te[c], 0, f)),
                  pl.BlockSpec((1, tf, D_MODEL), lambda c, f, te, tv: (te[c], f, 0))],
        out_specs=pl.BlockSpec((tm, D_MODEL), lambda c, f, te, tv: (c, 0)),
        scratch_shapes=[pltpu.VMEM((tm, D_MODEL), BF16), pltpu.VMEM((tm, D_MODEL), F32)],
    )
    return pl.pallas_call(
        _gmm_kernel,
        grid_spec=grid_spec,
        out_shape=jax.ShapeDtypeStruct((n_rows, D_MODEL), F32),
        compiler_params=_cparams(("parallel", "arbitrary")),
        name="moe_grouped_swiglu",
    )(tile_expert, tile_valid, xs, w_gate.astype(BF16), w_up.astype(BF16), w_down.astype(BF16))


def _combine_kernel(dest_ref, y_ref, gate_ref, h_ref, o_ref, buf_ref, sem):
    tc = h_ref.shape[0]

    def issue(i, carry):
        for k in range(TOP_K):
            _row_copy(y_ref, dest_ref[0, 0, TOP_K * i + k], buf_ref.at[k], i, sem).start()
        return carry

    lax.fori_loop(0, tc, issue, 0, unroll=DMA_ISSUE_UNROLL)
    for k in range(TOP_K):
        pltpu.make_async_copy(y_ref.at[pl.ds(0, tc)], buf_ref.at[k], sem).wait()
    g = gate_ref[...]
    o_ref[...] = h_ref[...] + g[:, 0:1] * buf_ref[0] + g[:, 1:2] * buf_ref[1]


def _combine(y, dest, gates, h):
    n = h.shape[0]
    tc = COMBINE_TOKENS
    return pl.pallas_call(
        _combine_kernel,
        grid=(n // tc,),
        in_specs=[pl.BlockSpec((1, 1, tc * TOP_K), lambda i: (i, 0, 0), memory_space=pltpu.SMEM),
                  pl.BlockSpec(memory_space=pl.ANY),
                  pl.BlockSpec((tc, TOP_K), lambda i: (i, 0)),
                  pl.BlockSpec((tc, D_MODEL), lambda i: (i, 0))],
        out_specs=pl.BlockSpec((tc, D_MODEL), lambda i: (i, 0)),
        out_shape=jax.ShapeDtypeStruct((n, D_MODEL), F32),
        scratch_shapes=[pltpu.VMEM((TOP_K, tc, D_MODEL), F32), pltpu.SemaphoreType.DMA(())],
        compiler_params=_cparams(("arbitrary",)),
        name="moe_combine",
    )(dest.reshape(n // tc, 1, tc * TOP_K), y, gates, h)


def _moe(v, route, h, w_gate, w_up, w_down):
    n = h.shape[0]
    nk = n * TOP_K
    tm = TM_MOE
    e_flat = route[:TOP_K].T.astype(jnp.int32).reshape(nk)
    gates = route[TOP_K:2 * TOP_K].T
    onehot = (e_flat[:, None] == jnp.arange(N_EXPERTS)[None, :]).astype(jnp.int32)
    csum = jnp.cumsum(onehot, axis=0)
    counts = csum[-1]
    rank = jnp.sum(jnp.where(onehot > 0, csum, 0), axis=1) - 1
    padded = (counts + tm - 1) // tm * tm
    pend = jnp.cumsum(padded)
    pstart = pend - padded
    dest = (jnp.sum(jnp.where(onehot > 0, pstart[None, :], 0), axis=1) + rank).astype(jnp.int32)
    n_tiles = nk // tm + N_EXPERTS
    tile_start = jnp.arange(n_tiles) * tm
    tile_expert = jnp.minimum(jnp.sum(tile_start[:, None] >= pend[None, :], axis=1), N_EXPERTS - 1)
    tile_valid = (tile_start < pend[-1]).astype(jnp.int32)
    xs = _dispatch(v, dest, n_tiles * tm)
    y = _grouped_swiglu(xs, tile_expert.astype(jnp.int32), tile_valid, w_gate, w_up, w_down)
    return _combine(y, dest, gates, h)


def kernel(x, rel_bias, attn_norm, w_in, nsa_q_norm, nsa_k_norm, cmp_pos, cmp_w1, cmp_b1, cmp_w2,
           dil_q_norm, dil_k_norm, out_norm, w_out, ffn_norm, ffn_w_gate, ffn_w_up, ffn_w_down,
           router_w, exp_w_gate, exp_w_up, exp_w_down):
    batch, seq, _ = x.shape
    depth = attn_norm.shape[0]
    n = batch * seq
    assert all((seq // d) % min(DIL_ROWS, seq // d) == 0 and (seq // d) % DIL_BAND == 0 for _, d in DIL_PAIRS)
    assert seq % TM_PROJ == 0 and n % DISPATCH_TOKENS == 0
    assert all(w == d * DIL_BAND for w, d in DIL_PAIRS)

    tbl = rel_bias.astype(F32).T
    tbl = tbl * LOG2E
    tbl_a, tbl_b = tbl[:N_HEADS_A], tbl[N_HEADS_A:]
    max_off = seq // TOEP - 1
    far_sel = min(max_off, -(-(FAR_DIST + TOEP - 1) // TOEP))
    far_win = min(max_off, -(-(WIN_A + TOEP - 1) // TOEP))
    bias_c = _bias_table(tbl_a, seq // TQ_NSA, TQ_NSA, seq // CMP_STRIDE,
                         functools.partial(_cmp_dist, n_cmp=(seq - CMP_LEN) // CMP_STRIDE + 1))
    bias_s = _bias_table(tbl_a, far_sel + 2, TOEP, TOEP, _sel_dist)
    bias_w = _bias_table(tbl_a, far_win + 2, TOEP, TOEP, _win_dist)
    bias_d = [_bias_table(tbl_b, 1, DIL_BAND, 2 * DIL_BAND, functools.partial(_dil_dist, dil=d))
              for _, d in DIL_PAIRS]

    h = x.reshape(n, D_MODEL)
    for layer in range(depth):
        qa, cva, ksw, ga, qkv_dilated = _in_projection(
            h, seq, attn_norm[layer], w_in[layer], nsa_q_norm[layer], nsa_k_norm[layer],
            dil_q_norm[layer], dil_k_norm[layer])
        kc, vc = _compress(cva, cmp_pos[layer], cmp_w1[layer], cmp_b1[layer], cmp_w2[layer],
                           nsa_k_norm[layer, 0])
        o_a = _nsa_attention(qa, kc, vc, ksw, ga, bias_c, bias_s, bias_w, batch, seq)
        dil_outs = [_dilated_attention(*qkv_dilated[i], bias_d[i], d) for i, (_, d) in enumerate(DIL_PAIRS)]
        moe_layer = layer % 2 == 1
        outs = _out_projection(o_a, dil_outs, h, out_norm[layer], w_out[layer], ffn_norm[layer],
                               router_w[layer // 2] if moe_layer else None)
        if moe_layer:
            h1, v, route = outs
            h = _moe(v, route, h1, exp_w_gate[layer // 2], exp_w_up[layer // 2], exp_w_down[layer // 2])
        else:
            h1, v = outs
            h = _dense_ffn(v, h1, ffn_w_gate[layer // 2], ffn_w_up[layer // 2], ffn_w_down[layer // 2])
    return h.reshape(batch, seq, D_MODEL)
```

```python
import functools
import math

import jax
import jax.numpy as jnp
import numpy as np
from jax import lax
from jax.experimental import pallas as pl
from jax.experimental.pallas import tpu as pltpu

F32 = jnp.float32
BF16 = jnp.bfloat16

D_MODEL = 1024
HEAD_DIM = 64
N_HEADS_A = 8
N_KV_A = 2
HPG_A = N_HEADS_A // N_KV_A
N_HEADS_B = 8
WIDTH_A = N_HEADS_A * HEAD_DIM
WIDTH_B = N_HEADS_B * HEAD_DIM
KV_A = N_KV_A * HEAD_DIM
CMP_LEN = 32
CMP_STRIDE = 16
CMP_HIDDEN = 256
SEL_BLOCK = 64
SEL_TOPK = 16
WIN_A = 512
DIL_PAIRS = ((128, 1), (512, 4), (2048, 16))
N_BUCKETS = 32
MAX_DISTANCE = 2048
N_EXPERTS = 8
TOP_K = 2
RMS_EPS = 1e-6
NEG = -1e30
SCALE = HEAD_DIM ** -0.5
LOG2E = math.log2(math.e)

LANES = 128
MXU_DIM = 256
VMEM_LIMIT = 56 * 1024 * 1024

TM_PROJ = 512
TQ_NSA = 256
TK_NSA = 512
TOEP = 128
DIL_BAND = 128
DIL_ROWS = 512
TM_FFN = 512
TF_FFN = 1408
TM_MOE = 1024
TF_MOE = 512
DISPATCH_TOKENS = 512
COMBINE_TOKENS = 256
DMA_ISSUE_UNROLL = 8

COL_QA = 0
COL_CVA = 512
COL_KSW = 768
COL_QB = 1280
COL_KB = 1792
COL_VB = 2304
COL_GA = 2816
IN_COLS = 2944
PAIRED_HEADS_A = np.array([h + g * HPG_A for h in range(HPG_A) for g in range(N_KV_A)])


def _cparams(sem, vmem=VMEM_LIMIT):
    return pltpu.CompilerParams(dimension_semantics=sem, vmem_limit_bytes=vmem)


def _dot(a, b):
    return jnp.dot(a, b, preferred_element_type=F32)


def _dot_nt(a, b):
    return lax.dot_general(a, b, (((1,), (1,)), ((), ())), preferred_element_type=F32)


def _split_bf16(x):
    hi = x.astype(BF16)
    lo = (x - hi.astype(F32)).astype(BF16)
    return hi, lo


def _bucket_thresholds():
    d = np.arange(0, 4 * MAX_DISTANCE, dtype=np.int64)
    max_exact = N_BUCKETS // 2
    scaled = np.log(np.maximum(d, 1).astype(np.float32) / np.float32(max_exact)) / np.float32(
        math.log(MAX_DISTANCE / max_exact))
    large = np.minimum(max_exact + (scaled.astype(np.float32) * (N_BUCKETS - max_exact)).astype(np.int32),
                       N_BUCKETS - 1)
    bucket = np.where(d < max_exact, d, large)
    assert np.all(np.diff(bucket) >= 0)
    return [int(np.argmax(bucket >= b)) for b in range(N_BUCKETS)]


_THR = _bucket_thresholds()
FAR_DIST = _THR[N_BUCKETS - 1]


def _bias_table_kernel(tbl_ref, out_ref, *, n_heads, rows, cols, dist_valid):
    i = pl.program_id(0)
    a = lax.broadcasted_iota(jnp.int32, (rows, cols), 0)
    c = lax.broadcasted_iota(jnp.int32, (rows, cols), 1)
    d, valid = dist_valid(i, a, c)
    for h in range(n_heads):
        acc = jnp.full((rows, cols), tbl_ref[h, 0], F32)
        for b in range(1, N_BUCKETS):
            acc = jnp.where(d >= _THR[b], tbl_ref[h, b], acc)
        out_ref[h, 0] = jnp.where(valid, acc, NEG)


def _bias_table(tbl, n_tiles, rows, cols, dist_valid):
    n_heads = tbl.shape[0]
    return pl.pallas_call(
        functools.partial(_bias_table_kernel, n_heads=n_heads, rows=rows, cols=cols, dist_valid=dist_valid),
        grid=(n_tiles,),
        in_specs=[pl.BlockSpec(memory_space=pltpu.SMEM)],
        out_specs=pl.BlockSpec((n_heads, 1, rows, cols), lambda i: (0, i, 0, 0)),
        out_shape=jax.ShapeDtypeStruct((n_heads, n_tiles, rows, cols), F32),
        compiler_params=_cparams(("arbitrary",)),
        name="bias_table",
    )(tbl)


def _cmp_dist(i, a, c, *, n_cmp):
    d = i * TQ_NSA + a - (c * CMP_STRIDE + CMP_LEN - 1)
    return d, (d >= 0) & (c < n_cmp)


def _sel_dist(i, a, c):
    d = (i - 1) * TOEP + a - c
    return d, d >= 0


def _win_dist(i, a, c):
    d = (i - 1) * TOEP + a - c
    return d, (d >= 0) & (d < WIN_A)


def _dil_dist(i, a, c, *, dil):
    n = DIL_BAND + a - c
    return n * dil, (n >= 0) & (n <= DIL_BAND)


def _inproj_kernel(h_ref, gn_ref, w_ref, gain_ref, bd_ref,
                   qa_ref, cva_ref, ksw_ref, ga_ref, *rest):
    dil_refs, stage_ref = rest[:-1], rest[-1]
    tm = h_ref.shape[0]

    def emit_dilated(y, which):
        dil_refs[which][0, 0] = y.astype(BF16)
        n_chunks = y.shape[1] // LANES
        for c in range(n_chunks):
            stage_ref[c] = y[:, c * LANES:(c + 1) * LANES]
        for d, (_, r) in enumerate(DIL_PAIRS):
            if r == 1:
                continue
            ref = dil_refs[3 * d + which]
            for rho in range(r):
                for c in range(n_chunks):
                    ref[0, rho, :, c * LANES:(c + 1) * LANES] = stage_ref[
                        c, pl.ds(rho, tm // r, stride=r), :].astype(BF16)

    x = h_ref[...]
    u = (x * lax.rsqrt(jnp.mean(x * x, axis=-1, keepdims=True) + RMS_EPS) * gn_ref[...]).astype(BF16)

    def proj(c0, width):
        return _dot(u, w_ref[:, c0:c0 + width])

    def headnorm(acc, c0):
        outs = []
        for j in range(acc.shape[1] // MXU_DIM):
            a = acc[:, j * MXU_DIM:(j + 1) * MXU_DIM]
            sq_hi, sq_lo = _split_bf16(a * a)
            ms = _dot(sq_hi, bd_ref[...]) + _dot(sq_lo, bd_ref[...])
            g = gain_ref[:, c0 + j * MXU_DIM:c0 + (j + 1) * MXU_DIM]
            outs.append(a * lax.rsqrt(ms + RMS_EPS) * g)
        return outs[0] if len(outs) == 1 else jnp.concatenate(outs, axis=1)

    qa_ref[...] = headnorm(proj(COL_QA, WIDTH_A), COL_QA).astype(BF16)
    cva = proj(COL_CVA, 2 * KV_A)
    for c in range(2 * KV_A // LANES):
        stage_ref[c] = cva[:, c * LANES:(c + 1) * LANES]
    for l in range(CMP_STRIDE):
        for c in range(2 * KV_A // LANES):
            cva_ref[0, :, l * 2 * KV_A + c * LANES:l * 2 * KV_A + (c + 1) * LANES] = stage_ref[
                c, pl.ds(l, tm // CMP_STRIDE, stride=CMP_STRIDE), :]
    ksw = proj(COL_KSW, 4 * KV_A)
    ksw_ref[:, :2 * KV_A] = headnorm(ksw[:, :2 * KV_A], COL_KSW).astype(BF16)
    ksw_ref[:, 2 * KV_A:] = ksw[:, 2 * KV_A:].astype(BF16)
    emit_dilated(headnorm(proj(COL_QB, WIDTH_B), COL_QB), 0)
    emit_dilated(headnorm(proj(COL_KB, WIDTH_B), COL_KB), 1)
    emit_dilated(proj(COL_VB, WIDTH_B), 2)
    ga_ref[...] = jax.nn.sigmoid(proj(COL_GA, LANES))


def _in_projection(h, seq, attn_norm, w_in, nsa_q_norm, nsa_k_norm, dil_q_norm, dil_k_norm):
    n = h.shape[0]
    o = np.cumsum((0, WIDTH_A, KV_A, KV_A, KV_A, KV_A, KV_A, KV_A, N_HEADS_A * 3, WIDTH_B, WIDTH_B, WIDTH_B))
    seg = [w_in[:, o[i]:o[i + 1]] for i in range(11)]
    qa, kc, vc, ks, vs, kw, vw, ga, qb, kb, vb = seg
    qa = qa.reshape(D_MODEL, N_HEADS_A, HEAD_DIM)[:, PAIRED_HEADS_A, :].reshape(D_MODEL, WIDTH_A)
    pad = jnp.zeros((D_MODEL, IN_COLS - COL_GA - N_HEADS_A * 3), w_in.dtype)
    w = jnp.concatenate([qa, kc, vc, ks, kw, vs, vw, qb, kb, vb, ga, pad], axis=1).astype(BF16)
    ones = jnp.ones((IN_COLS,), F32)
    gain = ones
    gain = gain.at[COL_QA:COL_QA + WIDTH_A].set(jnp.tile(nsa_q_norm, N_HEADS_A) * (SCALE * LOG2E))
    gain = gain.at[COL_KSW:COL_KSW + KV_A].set(jnp.tile(nsa_k_norm[1], N_KV_A))
    gain = gain.at[COL_KSW + KV_A:COL_KSW + 2 * KV_A].set(jnp.tile(nsa_k_norm[2], N_KV_A))
    gain = gain.at[COL_QB:COL_QB + WIDTH_B].set(jnp.tile(dil_q_norm, N_HEADS_B) * (SCALE * LOG2E))
    gain = gain.at[COL_KB:COL_KB + WIDTH_B].set(jnp.tile(dil_k_norm, N_HEADS_B))
    blk = np.arange(MXU_DIM) // HEAD_DIM
    bd = jnp.asarray((blk[:, None] == blk[None, :]).astype(np.float32) / HEAD_DIM, BF16)

    tm = TM_PROJ
    tiles_per_seq = seq // tm
    row = lambda width: pl.BlockSpec((tm, width), lambda i: (i, 0))
    full = lambda a: pl.BlockSpec(a.shape, lambda i: (0,) * a.ndim)
    gn = attn_norm.reshape(1, D_MODEL)
    gain = gain.reshape(1, IN_COLS)
    dil_specs, dil_shapes = [], []
    for _, r in DIL_PAIRS:
        spec = pl.BlockSpec((1, r, tm // r, WIDTH_B), lambda i: (i // tiles_per_seq, 0, i % tiles_per_seq, 0))
        dil_specs += [spec] * 3
        dil_shapes += [jax.ShapeDtypeStruct((n // seq, r, seq // r, WIDTH_B), BF16)] * 3
    outs = pl.pallas_call(
        _inproj_kernel,
        grid=(n // tm,),
        in_specs=[row(D_MODEL), full(gn), full(w), full(gain), full(bd)],
        out_specs=[row(WIDTH_A),
                   pl.BlockSpec((1, tm // CMP_STRIDE, CMP_STRIDE * 2 * KV_A),
                                lambda i: (i // tiles_per_seq, i % tiles_per_seq, 0)),
                   row(4 * KV_A), row(LANES)] + dil_specs,
        out_shape=[jax.ShapeDtypeStruct((n, WIDTH_A), BF16),
                   jax.ShapeDtypeStruct((n // seq, seq // CMP_STRIDE, CMP_STRIDE * 2 * KV_A), F32),
                   jax.ShapeDtypeStruct((n, 4 * KV_A), BF16), jax.ShapeDtypeStruct((n, LANES), F32)] + dil_shapes,
        scratch_shapes=[pltpu.VMEM((WIDTH_B // LANES, tm, LANES), F32)],
        compiler_params=_cparams(("parallel",)),
        name="in_projection",
    )(h, gn, w, gain, bd)
    qa, cva, ksw, ga = outs[:4]
    qkv_dilated = [outs[4 + 3 * d:7 + 3 * d] for d in range(len(DIL_PAIRS))]
    return qa, cva, ksw, ga, qkv_dilated


def _gelu_tanh(x):
    return 0.5 * x * (1.0 + jnp.tanh(math.sqrt(2.0 / math.pi) * (x + 0.044715 * (x * x * x))))


def _compress_kernel(x_ref, pos_ref, w1_ref, b1_ref, w2_ref, kg_ref, kc_ref, vc_ref):
    rows = x_ref.shape[1]
    half = CMP_LEN // 2
    for which, out_ref in ((0, kc_ref), (1, vc_ref)):
        top = jnp.zeros((rows, 2 * CMP_HIDDEN), F32)
        bot = jnp.zeros((rows, 2 * CMP_HIDDEN), F32)
        for l in range(half):
            c0 = l * 2 * KV_A + which * KV_A
            a = x_ref[0, :, c0:c0 + KV_A]
            top += _dot((a + pos_ref[which, l:l + 1, :]).astype(BF16), w1_ref[which, l])
            bot += _dot((a + pos_ref[which, half + l:half + l + 1, :]).astype(BF16), w1_ref[which, half + l])
        hid = top + pltpu.roll(bot, rows - 1, axis=0) + b1_ref[which]
        y = _dot(_gelu_tanh(hid).astype(BF16), w2_ref[which])
        if which == 0:
            parts = []
            for g in range(N_KV_A):
                yg = y[:, g * HEAD_DIM:(g + 1) * HEAD_DIM]
                parts.append(yg * lax.rsqrt(jnp.mean(yg * yg, axis=-1, keepdims=True) + RMS_EPS))
            y = jnp.concatenate(parts, axis=1) * kg_ref[...]
        out_ref[0] = y.astype(BF16)


def _compress(x, cmp_pos, cmp_w1, cmp_b1, cmp_w2, k_norm0):
    batch, rows, _ = x.shape
    pos = jnp.tile(cmp_pos, (1, 1, N_KV_A))
    w1 = cmp_w1.reshape(2, CMP_LEN, HEAD_DIM, CMP_HIDDEN).astype(BF16)
    z1 = jnp.zeros_like(w1)
    w1 = jnp.concatenate([jnp.concatenate([w1, z1], axis=3), jnp.concatenate([z1, w1], axis=3)], axis=2)
    b1 = jnp.tile(cmp_b1, (1, N_KV_A)).reshape(2, 1, 2 * CMP_HIDDEN)
    w2 = cmp_w2.astype(BF16)
    z2 = jnp.zeros_like(w2)
    w2 = jnp.concatenate([jnp.concatenate([w2, z2], axis=2), jnp.concatenate([z2, w2], axis=2)], axis=1)
    kg = jnp.tile(k_norm0, N_KV_A).reshape(1, KV_A)
    full = lambda a: pl.BlockSpec(a.shape, lambda b: (0,) * a.ndim)
    out = pl.BlockSpec((1, rows, KV_A), lambda b: (b, 0, 0))
    return pl.pallas_call(
        _compress_kernel,
        grid=(batch,),
        in_specs=[pl.BlockSpec((1, rows, x.shape[2]), lambda b: (b, 0, 0)),
                  full(pos), full(w1), full(b1), full(w2), full(kg)],
        out_specs=[out, out],
        out_shape=[jax.ShapeDtypeStruct((batch, rows, KV_A), BF16)] * 2,
        compiler_params=_cparams(("parallel",)),
        name="nsa_compress",
    )(x, pos, w1, b1, w2, kg)


def _toeplitz_bias(tbl_ref, g, base, n_a, n_c):
    far = tbl_ref.shape[1] - 2
    rows = []
    for a in range(n_a):
        tiles = [tbl_ref[g * HPG_A:(g + 1) * HPG_A, jnp.clip(base + a - c, -1, far) + 1] for c in range(n_c)]
        rows.append(jnp.concatenate(tiles, axis=2))
    return jnp.concatenate(rows, axis=1)


def _nsa_kernel(q_ref, kc_ref, vc_ref, ksw_ref, ga_ref, bc_ref, bs_ref, bw_ref, ex_ref, gp_ref, o_ref,
                imp_ref, *, seq):
    tq, tk = TQ_NSA, TK_NSA
    qi = pl.program_id(1)
    rows_c = seq // CMP_STRIDE
    n_sb = seq // SEL_BLOCK
    k_sel = min(SEL_TOPK, n_sb)
    rows = HPG_A * tq
    groups = range(N_KV_A)

    lane = lax.broadcasted_iota(jnp.int32, (1, LANES), 1)
    low_half = lane < HEAD_DIM
    half_bf = [jnp.where(low_half, 1.0, 0.0).astype(BF16), jnp.where(low_half, 0.0, 1.0).astype(BF16)]
    qs = [jnp.concatenate([q_ref[:, c * LANES:(c + 1) * LANES] * half_bf[g] for c in range(HPG_A)], axis=0)
          for g in groups]

    def row_sums(p_bf):
        return _dot(p_bf, jnp.ones((p_bf.shape[1], LANES), BF16))

    jj = lax.broadcasted_iota(jnp.int32, (n_sb, rows_c), 0) * SEL_BLOCK
    nn = lax.broadcasted_iota(jnp.int32, (n_sb, rows_c), 1) * CMP_STRIDE
    ov = jnp.maximum(jnp.minimum(nn + CMP_LEN, jj + SEL_BLOCK) - jnp.maximum(nn, jj), 0)
    ov_t = (ov.astype(F32) * (1.0 / CMP_LEN)).astype(BF16)
    blk = lax.broadcasted_iota(jnp.int32, (n_sb, tq), 0)
    tpos = qi * tq + lax.broadcasted_iota(jnp.int32, (n_sb, tq), 1)
    cur = tpos // SEL_BLOCK
    forced = (blk == 0) | (blk == cur) | (blk == cur - 1)
    future = blk * SEL_BLOCK > tpos

    o_c, unsel_bf = [], []
    for g in groups:
        s = _dot_nt(qs[g], kc_ref[0]).reshape(HPG_A, tq, rows_c) + bc_ref[g * HPG_A:(g + 1) * HPG_A, 0]
        m = jnp.max(s, axis=-1, keepdims=True)
        e = jnp.where(s > 0.5 * NEG, jnp.exp2(s - m), 0.0)
        e_bf = e.reshape(rows, rows_c).astype(BF16)
        inv = 1.0 / jnp.maximum(row_sums(e_bf), 1e-30)
        o_c.append(_dot(e_bf, vc_ref[0]) * inv)
        inv4 = inv.reshape(HPG_A, tq, LANES)
        p = e * jnp.concatenate([inv4] * (rows_c // LANES), axis=2)
        p_sum = p[0] + p[1] + p[2] + p[3]
        p_hi, p_lo = _split_bf16(p_sum)
        p_lo2 = (p_sum - p_hi.astype(F32) - p_lo.astype(F32)).astype(BF16)
        imp = _dot_nt(ov_t, p_hi) + _dot_nt(ov_t, p_lo) + _dot_nt(ov_t, p_lo2)
        imp = jnp.where(forced, 1e6, jnp.where(future, -1e6, imp))
        imp_ref[...] = imp

        def count_beaten(i0, cnt):
            for u in range(tq // SEL_BLOCK):
                i = i0 * (tq // SEL_BLOCK) + u
                ri = jnp.broadcast_to(imp_ref[pl.ds(i, 1), :], (n_sb, tq))
                later = jnp.where(blk > i, 1.0, 0.0)
                cnt = cnt + jnp.where(ri > imp, 1.0, jnp.where(ri == imp, later, 0.0))
            return cnt

        cnt = lax.fori_loop(0, qi + 1, count_beaten, jnp.zeros((n_sb, tq), F32))
        unsel_bf.append(jnp.where(cnt < k_sel, 0.0, NEG).T.astype(BF16))

    def with_ones(v_pair, g):
        return v_pair * half_bf[g] + half_bf[1 - g]

    def normalised(acc):
        return acc / pltpu.roll(acc, HEAD_DIM, axis=1)

    def sel_body(kj, carry):
        r0 = pl.multiple_of(kj * tk, tk)
        k = ksw_ref[pl.ds(r0, tk), 0:KV_A]
        v = ksw_ref[pl.ds(r0, tk), 2 * KV_A:3 * KV_A]
        out = []
        for g in groups:
            m, acc = carry[g]
            s = _dot_nt(qs[g], k).reshape(HPG_A, tq, tk)
            s = s + _toeplitz_bias(bs_ref, g, (tq // TOEP) * qi - (tk // TOEP) * kj, tq // TOEP, tk // TOEP)
            madd = _dot(unsel_bf[g], ex_ref[kj])
            s = (s + madd[None]).reshape(rows, tk)
            m_new = jnp.maximum(m, jnp.max(s, axis=-1, keepdims=True))
            p = jnp.exp2(s - m_new).astype(BF16)
            out.append((m_new, jnp.exp2(m - m_new) * acc + _dot(p, with_ones(v, g))))
        return tuple(out)

    init = tuple((jnp.full((rows, 1), NEG, F32), jnp.zeros((rows, LANES), F32)) for _ in groups)
    n_tiles = (qi * tq + tq + tk - 1) // tk
    sel_out = lax.fori_loop(0, n_tiles, sel_body, init)
    o_s = [normalised(acc) for _, acc in sel_out]

    n_wk = WIN_A + tq
    start = pl.multiple_of(jnp.maximum(qi * tq - WIN_A, 0), tq)
    kw = ksw_ref[pl.ds(start, n_wk), KV_A:2 * KV_A]
    vw = ksw_ref[pl.ds(start, n_wk), 3 * KV_A:4 * KV_A]
    o_w = []
    for g in groups:
        s = _dot_nt(qs[g], kw).reshape(HPG_A, tq, n_wk)
        s = s + _toeplitz_bias(bw_ref, g, (qi * tq - start) // TOEP, tq // TOEP, n_wk // TOEP)
        s = s.reshape(rows, n_wk)
        p = jnp.exp2(s - jnp.max(s, axis=-1, keepdims=True)).astype(BF16)
        o_w.append(normalised(_dot(p, with_ones(vw, g))))

    g_hi, g_lo = _split_bf16(ga_ref[...])
    gates = _dot(g_hi, gp_ref[...]) + _dot(g_lo, gp_ref[...])
    for c in range(HPG_A):
        rs = slice(c * tq, (c + 1) * tq)
        out = jnp.zeros((tq, LANES), F32)
        for j, o in enumerate((o_c, o_s, o_w)):
            gate = gates[:, j * WIDTH_A + c * LANES:j * WIDTH_A + (c + 1) * LANES]
            out = out + gate * jnp.where(low_half, o[0][rs], o[1][rs])
        o_ref[:, c * LANES:(c + 1) * LANES] = out


def _nsa_attention(qa, kc, vc, ksw, ga, bias_c, bias_s, bias_w, batch, seq):
    tq, tk = TQ_NSA, TK_NSA
    nq = seq // tq
    n = batch * seq
    rows_c = seq // CMP_STRIDE
    n_sb = seq // SEL_BLOCK
    assert rows_c % LANES == 0 and seq % tk == 0 and seq >= WIN_A + tq and WIN_A % tq == 0
    key_blk = (np.arange(seq) // SEL_BLOCK).reshape(seq // tk, 1, tk)
    expand = jnp.asarray((key_blk == np.arange(n_sb).reshape(1, n_sb, 1)).astype(np.float32), BF16)
    gp = np.zeros((LANES, 3 * WIDTH_A), np.float32)
    for pos, head in enumerate(PAIRED_HEADS_A):
        for j in range(3):
            gp[3 * head + j, j * WIDTH_A + pos * HEAD_DIM:j * WIDTH_A + (pos + 1) * HEAD_DIM] = 1.0
    gp = jnp.asarray(gp, BF16)
    resident = lambda a: pl.BlockSpec(a.shape, lambda b, i: (0,) * a.ndim, pipeline_mode=pl.Buffered(1))
    return pl.pallas_call(
        functools.partial(_nsa_kernel, seq=seq),
        grid=(batch, nq),
        in_specs=[
            pl.BlockSpec((tq, WIDTH_A), lambda b, i: (b * nq + i, 0)),
            pl.BlockSpec((1, rows_c, KV_A), lambda b, i: (b, 0, 0)),
            pl.BlockSpec((1, rows_c, KV_A), lambda b, i: (b, 0, 0)),
            pl.BlockSpec((seq, 4 * KV_A), lambda b, i: (b, 0)),
            pl.BlockSpec((tq, LANES), lambda b, i: (b * nq + i, 0)),
            pl.BlockSpec((N_HEADS_A, 1, tq, rows_c), lambda b, i: (0, i, 0, 0)),
            resident(bias_s), resident(bias_w), resident(expand), resident(gp),
        ],
        out_specs=pl.BlockSpec((tq, WIDTH_A), lambda b, i: (b * nq + i, 0)),
        out_shape=jax.ShapeDtypeStruct((n, WIDTH_A), F32),
        scratch_shapes=[pltpu.VMEM((n_sb, tq), F32)],
        compiler_params=_cparams(("parallel", "arbitrary")),
        name="nsa_attention",
    )(qa, kc, vc, ksw, ga, bias_c, bias_s, bias_w, expand, gp)


def _dilated_kernel(q_ref, kp_ref, kc_ref, vp_ref, vc_ref, bias_ref, o_ref, lse_ref):
    band = DIL_BAND
    first = pl.program_id(2) == 0
    prev_mask = jnp.where(first, NEG, 0.0)
    lane = lax.broadcasted_iota(jnp.int32, (1, LANES), 1)
    low_half = lane < HEAD_DIM
    half_bf = [jnp.where(low_half, 1.0, 0.0).astype(BF16), jnp.where(low_half, 0.0, 1.0).astype(BF16)]
    den_sel = [jnp.broadcast_to(hm, (2 * band, LANES)) for hm in half_bf]
    key_col = lax.broadcasted_iota(jnp.int32, (1, 2 * band), 1)
    prev_cols = jnp.where(key_col < band, prev_mask, 0.0)
    n_cls, n_sub = q_ref.shape[0], q_ref.shape[1] // band
    for cl, sub in ((cl, sub) for cl in range(n_cls) for sub in range(n_sub)):
        r_cur = slice(sub * band, (sub + 1) * band)
        for c in range(WIDTH_B // LANES):
            cols = slice(c * LANES, (c + 1) * LANES)
            q = q_ref[cl, r_cur, cols]
            if sub == 0:
                k_cat = jnp.concatenate([kp_ref[cl, :, cols], kc_ref[cl, r_cur, cols]], axis=0)
                v_cat = jnp.concatenate([vp_ref[cl, :, cols], vc_ref[cl, r_cur, cols]], axis=0)
            else:
                k_cat = kc_ref[cl, (sub - 1) * band:(sub + 1) * band, cols]
                v_cat = vc_ref[cl, (sub - 1) * band:(sub + 1) * band, cols]
            acc = jnp.zeros((band, 2 * LANES), F32)
            ms = []
            for j in range(2):
                h = 2 * c + j
                s = _dot_nt(q * half_bf[j], k_cat) + bias_ref[h, 0]
                if sub == 0:
                    s = s + prev_cols
                m = jnp.max(s, axis=-1, keepdims=True)
                e = jnp.exp2(s - m).astype(BF16)
                acc = acc + _dot(e, jnp.concatenate([v_cat * half_bf[j], den_sel[j]], axis=1))
                ms.append(m)
            den = acc[:, LANES:]
            o_ref[cl, r_cur, cols] = acc[:, :LANES] / den
            lse_ref[cl, r_cur, cols] = jnp.where(low_half, ms[0], ms[1]) + jnp.log2(den)


def _dilated_attention(q, k, v, bias, dil):
    band = DIL_BAND
    batch, _, length, _ = q.shape
    rows = min(DIL_ROWS, length)
    n_cls = min(dil, DIL_ROWS // rows)
    sub = rows // band
    cur = pl.BlockSpec((None, n_cls, rows, WIDTH_B), lambda b, r, n: (b, r, n, 0))
    prev = pl.BlockSpec((None, n_cls, band, WIDTH_B), lambda b, r, n: (b, r, jnp.maximum(n * sub - 1, 0), 0))
    return pl.pallas_call(
        _dilated_kernel,
        grid=(batch, dil // n_cls, length // rows),
        in_specs=[cur, prev, cur, prev, cur,
                  pl.BlockSpec(bias.shape, lambda b, r, n: (0, 0, 0, 0))],
        out_specs=[cur, cur],
        out_shape=[jax.ShapeDtypeStruct((batch, dil, length, WIDTH_B), F32)] * 2,
        compiler_params=_cparams(("parallel", "parallel", "arbitrary")),
        name=f"dilated_attention_d{dil}",
    )(q, k, k, v, v, bias)


def _outproj_kernel(*refs, with_router):
    if with_router:
        (oa_ref, o1_ref, o2_ref, o3_ref, l1_ref, l2_ref, l3_ref, h_ref, gn_ref, w_ref, fg_ref, rw_ref,
         h_out_ref, v_ref, route_ref, stage_ref) = refs
    else:
        (oa_ref, o1_ref, o2_ref, o3_ref, l1_ref, l2_ref, l3_ref, h_ref, gn_ref, w_ref, fg_ref,
         h_out_ref, v_ref, stage_ref) = refs
    tm = h_ref.shape[0]

    def token_order(ref, slot):
        r = ref.shape[1]
        if r == 1:
            return ref[0, 0]
        n_chunks = ref.shape[3] // LANES
        for rho in range(r):
            for c in range(n_chunks):
                stage_ref[slot, c, pl.ds(rho, tm // r, stride=r), :] = ref[0, rho, :, c * LANES:(c + 1) * LANES]
        return jnp.concatenate([stage_ref[slot, c] for c in range(n_chunks)], axis=1)

    l1, l2, l3 = token_order(l1_ref, 0), token_order(l2_ref, 0), token_order(l3_ref, 1)
    m = jnp.maximum(jnp.maximum(l1, l2), l3)
    e1, e2, e3 = jnp.exp2(l1 - m), jnp.exp2(l2 - m), jnp.exp2(l3 - m)
    o1, o2, o3 = token_order(o1_ref, 0), token_order(o2_ref, 2), token_order(o3_ref, 3)
    ob = (e1 * o1 + e2 * o2 + e3 * o3) / (e1 + e2 + e3)

    def norm(x, g):
        return x * lax.rsqrt(jnp.mean(x * x, axis=-1, keepdims=True) + RMS_EPS) * g

    o = jnp.concatenate([norm(oa_ref[...], gn_ref[:, :WIDTH_A]), norm(ob, gn_ref[:, WIDTH_A:])], axis=1)
    h = h_ref[...] + _dot(o.astype(BF16), w_ref[...])
    h_out_ref[...] = h
    v = norm(h, fg_ref[...])
    v_ref[...] = v.astype(v_ref.dtype)
    if with_router:
        v_hi, v_lo = _split_bf16(v)
        w_hi, w_lo = _split_bf16(rw_ref[...])
        logits = _dot(v_hi, w_hi) + _dot(v_lo, w_hi) + _dot(v_hi, w_lo)
        lg = logits.T[:N_EXPERTS]
        row = lax.broadcasted_iota(jnp.int32, lg.shape, 0).astype(F32)
        m1 = jnp.max(lg, axis=0, keepdims=True)
        i1 = jnp.min(jnp.where(lg == m1, row, float(N_EXPERTS)), axis=0, keepdims=True)
        lg2 = jnp.where(row == i1, -jnp.inf, lg)
        m2 = jnp.max(lg2, axis=0, keepdims=True)
        i2 = jnp.min(jnp.where(lg2 == m2, row, float(N_EXPERTS)), axis=0, keepdims=True)
        e = jnp.exp(m2 - m1)
        g1 = 1.0 / (1.0 + e)
        g2 = e / (1.0 + e)
        route_ref[...] = jnp.where(row == 0, i1, jnp.where(
            row == 1, i2, jnp.where(row == 2, g1, jnp.where(row == 3, g2, 0.0))))


def _out_projection(o_a, dil_outs, h, out_norm, w_out, ffn_norm, router_w):
    n = h.shape[0]
    tm = TM_PROJ
    with_router = router_w is not None
    row = lambda width: pl.BlockSpec((tm, width), lambda i: (i, 0))
    full = lambda a: pl.BlockSpec(a.shape, lambda i: (0,) * a.ndim)
    (o1, l1), (o2, l2), (o3, l3) = dil_outs
    gn_a = out_norm[:WIDTH_A].reshape(N_HEADS_A, HEAD_DIM)[PAIRED_HEADS_A].reshape(WIDTH_A)
    gn = jnp.concatenate([gn_a, out_norm[WIDTH_A:]]).reshape(1, -1)
    fg = ffn_norm.reshape(1, -1)
    w_a = w_out[:WIDTH_A].reshape(N_HEADS_A, HEAD_DIM, D_MODEL)[PAIRED_HEADS_A].reshape(WIDTH_A, D_MODEL)
    w = jnp.concatenate([w_a, w_out[WIDTH_A:]], axis=0).astype(BF16)
    tiles_per_seq = o1.shape[2] // tm
    grouped = lambda a: pl.BlockSpec((1, a.shape[1], tm // a.shape[1], WIDTH_B),
                                     lambda i: (i // tiles_per_seq, 0, i % tiles_per_seq, 0))
    args = [o_a, o1, o2, o3, l1, l2, l3, h, gn, w, fg]
    in_specs = ([row(WIDTH_A)] + [grouped(a) for a in args[1:7]]
                + [row(D_MODEL), full(gn), full(w), full(fg)])
    out_specs = [row(D_MODEL), row(D_MODEL)]
    out_shape = [jax.ShapeDtypeStruct((n, D_MODEL), F32),
                 jax.ShapeDtypeStruct((n, D_MODEL), F32 if with_router else BF16)]
    if with_router:
        rw = jnp.pad(router_w, ((0, 0), (0, LANES - N_EXPERTS)))
        args.append(rw)
        in_specs.append(full(rw))
        out_specs.append(pl.BlockSpec((N_EXPERTS, tm), lambda i: (0, i)))
        out_shape.append(jax.ShapeDtypeStruct((N_EXPERTS, n), F32))
    return pl.pallas_call(
        functools.partial(_outproj_kernel, with_router=with_router),
        grid=(n // tm,),
        in_specs=in_specs,
        out_specs=out_specs,
        out_shape=out_shape,
        scratch_shapes=[pltpu.VMEM((4, WIDTH_B // LANES, tm, LANES), F32)],
        compiler_params=_cparams(("parallel",)),
        name="out_projection",
    )(*args)


def _ffn_kernel(v_ref, h_ref, wg_ref, wu_ref, wd_ref, o_ref, acc_ref):
    f = pl.program_id(1)

    @pl.when(f == 0)
    def _():
        acc_ref[...] = jnp.zeros_like(acc_ref)

    v = v_ref[...]
    gate = _dot(v, wg_ref[...])
    up = _dot(v, wu_ref[...])
    acc_ref[...] += _dot((jax.nn.silu(gate) * up).astype(BF16), wd_ref[...])

    @pl.when(f == pl.num_programs(1) - 1)
    def _():
        o_ref[...] = h_ref[...] + acc_ref[...]


def _dense_ffn(v, h, w_gate, w_up, w_down):
    n = h.shape[0]
    d_ff = w_gate.shape[1]
    tm, tf = TM_FFN, TF_FFN
    return pl.pallas_call(
        _ffn_kernel,
        grid=(n // tm, d_ff // tf),
        in_specs=[pl.BlockSpec((tm, D_MODEL), lambda i, f: (i, 0)),
                  pl.BlockSpec((tm, D_MODEL), lambda i, f: (i, 0)),
                  pl.BlockSpec((D_MODEL, tf), lambda i, f: (0, f)),
                  pl.BlockSpec((D_MODEL, tf), lambda i, f: (0, f)),
                  pl.BlockSpec((tf, D_MODEL), lambda i, f: (f, 0))],
        out_specs=pl.BlockSpec((tm, D_MODEL), lambda i, f: (i, 0)),
        out_shape=jax.ShapeDtypeStruct((n, D_MODEL), F32),
        scratch_shapes=[pltpu.VMEM((tm, D_MODEL), F32)],
        compiler_params=_cparams(("parallel", "arbitrary")),
        name="dense_ffn",
    )(v, h, w_gate.astype(BF16), w_up.astype(BF16), w_down.astype(BF16))


def _row_copy(src_ref, src_row, dst_ref, dst_row, sem):
    return pltpu.make_async_copy(src_ref.at[pl.ds(src_row, 1)], dst_ref.at[pl.ds(dst_row, 1)], sem)


def _dispatch_kernel(dest_ref, v_ref, xs_in_ref, xs_ref, sem):
    del xs_in_ref
    n_slots = dest_ref.shape[2]

    def issue(i, carry):
        for k in range(TOP_K):
            _row_copy(v_ref, i, xs_ref, dest_ref[0, 0, TOP_K * i + k], sem).start()
        return carry

    lax.fori_loop(0, n_slots // TOP_K, issue, 0, unroll=DMA_ISSUE_UNROLL)
    pltpu.make_async_copy(xs_ref.at[pl.ds(0, n_slots)], xs_ref.at[pl.ds(0, n_slots)], sem).wait()


def _dispatch(v, dest, n_rows):
    n = v.shape[0]
    tokens = DISPATCH_TOKENS
    slots = tokens * TOP_K
    return pl.pallas_call(
        _dispatch_kernel,
        grid=(n // tokens,),
        in_specs=[pl.BlockSpec((1, 1, slots), lambda i: (i, 0, 0), memory_space=pltpu.SMEM),
                  pl.BlockSpec((tokens, D_MODEL), lambda i: (i, 0)),
                  pl.BlockSpec(memory_space=pl.ANY)],
        out_specs=pl.BlockSpec(memory_space=pl.ANY),
        out_shape=jax.ShapeDtypeStruct((n_rows, D_MODEL), F32),
        scratch_shapes=[pltpu.SemaphoreType.DMA(())],
        input_output_aliases={2: 0},
        compiler_params=_cparams(("arbitrary",)),
        name="moe_dispatch",
    )(dest.reshape(n // tokens, 1, slots), v, jnp.zeros((n_rows, D_MODEL), F32))


def _gmm_kernel(te_ref, tv_ref, x_ref, wg_ref, wu_ref, wd_ref, y_ref, xb_ref, acc_ref):
    c = pl.program_id(0)
    f = pl.program_id(1)
    valid = tv_ref[c] > 0

    @pl.when(f == 0)
    def _():
        acc_ref[...] = jnp.zeros_like(acc_ref)
        xb_ref[...] = x_ref[...].astype(BF16)

    @pl.when(valid)
    def _():
        x = xb_ref[...]
        gate = _dot(x, wg_ref[0])
        up = _dot(x, wu_ref[0])
        acc_ref[...] += _dot((jax.nn.silu(gate) * up).astype(BF16), wd_ref[0])

    @pl.when(f == pl.num_programs(1) - 1)
    def _():
        y_ref[...] = acc_ref[...]


def _grouped_swiglu(xs, tile_expert, tile_valid, w_gate, w_up, w_down):
    n_rows = xs.shape[0]
    d_ff = w_gate.shape[2]
    tm, tf = TM_MOE, TF_MOE
    grid_spec = pltpu.PrefetchScalarGridSpec(
        num_scalar_prefetch=2,
        grid=(n_rows // tm, d_ff // tf),
        in_specs=[pl.BlockSpec((tm, D_MODEL), lambda c, f, te, tv: (c, 0)),
                  pl.BlockSpec((1, D_MODEL, tf), lambda c, f, te, tv: (te[c], 0, f)),
                  pl.BlockSpec((1, D_MODEL, tf), lambda c, f, te, tv: (te[c], 0, f)),
                  pl.BlockSpec((1, tf, D_MODEL), lambda c, f, te, tv: (te[c], f, 0))],
        out_specs=pl.BlockSpec((tm, D_MODEL), lambda c, f, te, tv: (c, 0)),
        scratch_shapes=[pltpu.VMEM((tm, D_MODEL), BF16), pltpu.VMEM((tm, D_MODEL), F32)],
    )
    return pl.pallas_call(
        _gmm_kernel,
        grid_spec=grid_spec,
        out_shape=jax.ShapeDtypeStruct((n_rows, D_MODEL), F32),
        compiler_params=_cparams(("parallel", "arbitrary")),
        name="moe_grouped_swiglu",
    )(tile_expert, tile_valid, xs, w_gate.astype(BF16), w_up.astype(BF16), w_down.astype(BF16))


def _combine_kernel(dest_ref, dest_next_ref, y_ref, gate_ref, h_ref, o_ref, buf_ref, sem):
    tc = h_ref.shape[0]
    step = pl.program_id(0)
    slot = step % 2

    def start_gather(idx_ref, into):
        def issue(i, carry):
            for k in range(TOP_K):
                _row_copy(y_ref, idx_ref[0, 0, TOP_K * i + k], buf_ref.at[into, k], i, sem.at[into]).start()
            return carry

        lax.fori_loop(0, tc, issue, 0, unroll=DMA_ISSUE_UNROLL)

    @pl.when(step == 0)
    def _():
        start_gather(dest_ref, 0)

    @pl.when(step + 1 < pl.num_programs(0))
    def _():
        start_gather(dest_next_ref, 1 - slot)

    for k in range(TOP_K):
        pltpu.make_async_copy(y_ref.at[pl.ds(0, tc)], buf_ref.at[slot, k], sem.at[slot]).wait()
    g = gate_ref[...]
    o_ref[...] = h_ref[...] + g[:, 0:1] * buf_ref[slot, 0] + g[:, 1:2] * buf_ref[slot, 1]


def _combine(y, dest, gates, h):
    n = h.shape[0]
    tc = COMBINE_TOKENS
    steps = n // tc
    dest = dest.reshape(steps, 1, tc * TOP_K)
    return pl.pallas_call(
        _combine_kernel,
        grid=(steps,),
        in_specs=[pl.BlockSpec((1, 1, tc * TOP_K), lambda i: (i, 0, 0), memory_space=pltpu.SMEM),
                  pl.BlockSpec((1, 1, tc * TOP_K), lambda i: (jnp.minimum(i + 1, steps - 1), 0, 0),
                               memory_space=pltpu.SMEM),
                  pl.BlockSpec(memory_space=pl.ANY),
                  pl.BlockSpec((tc, TOP_K), lambda i: (i, 0)),
                  pl.BlockSpec((tc, D_MODEL), lambda i: (i, 0))],
        out_specs=pl.BlockSpec((tc, D_MODEL), lambda i: (i, 0)),
        out_shape=jax.ShapeDtypeStruct((n, D_MODEL), F32),
        scratch_shapes=[pltpu.VMEM((2, TOP_K, tc, D_MODEL), F32), pltpu.SemaphoreType.DMA((2,))],
        compiler_params=_cparams(("arbitrary",)),
        name="moe_combine",
    )(dest, dest, y, gates, h)


def _moe(v, route, h, w_gate, w_up, w_down):
    n = h.shape[0]
    nk = n * TOP_K
    tm = TM_MOE
    e_flat = route[:TOP_K].T.astype(jnp.int32).reshape(nk)
    gates = route[TOP_K:2 * TOP_K].T
    onehot = (e_flat[:, None] == jnp.arange(N_EXPERTS)[None, :]).astype(jnp.int32)
    csum = jnp.cumsum(onehot, axis=0)
    counts = csum[-1]
    rank = jnp.sum(jnp.where(onehot > 0, csum, 0), axis=1) - 1
    padded = (counts + tm - 1) // tm * tm
    pend = jnp.cumsum(padded)
    pstart = pend - padded
    dest = (jnp.sum(jnp.where(onehot > 0, pstart[None, :], 0), axis=1) + rank).astype(jnp.int32)
    n_tiles = nk // tm + N_EXPERTS
    tile_start = jnp.arange(n_tiles) * tm
    tile_expert = jnp.minimum(jnp.sum(tile_start[:, None] >= pend[None, :], axis=1), N_EXPERTS - 1)
    tile_valid = (tile_start < pend[-1]).astype(jnp.int32)
    xs = _dispatch(v, dest, n_tiles * tm)
    y = _grouped_swiglu(xs, tile_expert.astype(jnp.int32), tile_valid, w_gate, w_up, w_down)
    return _combine(y, dest, gates, h)


def kernel(x, rel_bias, attn_norm, w_in, nsa_q_norm, nsa_k_norm, cmp_pos, cmp_w1, cmp_b1, cmp_w2,
           dil_q_norm, dil_k_norm, out_norm, w_out, ffn_norm, ffn_w_gate, ffn_w_up, ffn_w_down,
           router_w, exp_w_gate, exp_w_up, exp_w_down):
    batch, seq, _ = x.shape
    depth = attn_norm.shape[0]
    n = batch * seq
    assert all((seq // d) % min(DIL_ROWS, seq // d) == 0 and (seq // d) % DIL_BAND == 0 for _, d in DIL_PAIRS)
    assert seq % TM_PROJ == 0 and n % DISPATCH_TOKENS == 0
    assert all(w == d * DIL_BAND for w, d in DIL_PAIRS)

    tbl = rel_bias.astype(F32).T
    tbl = tbl * LOG2E
    tbl_a, tbl_b = tbl[:N_HEADS_A], tbl[N_HEADS_A:]
    max_off = seq // TOEP - 1
    far_sel = min(max_off, -(-(FAR_DIST + TOEP - 1) // TOEP))
    far_win = min(max_off, -(-(WIN_A + TOEP - 1) // TOEP))
    bias_c = _bias_table(tbl_a, seq // TQ_NSA, TQ_NSA, seq // CMP_STRIDE,
                         functools.partial(_cmp_dist, n_cmp=(seq - CMP_LEN) // CMP_STRIDE + 1))
    bias_s = _bias_table(tbl_a, far_sel + 2, TOEP, TOEP, _sel_dist)
    bias_w = _bias_table(tbl_a, far_win + 2, TOEP, TOEP, _win_dist)
    bias_d = [_bias_table(tbl_b, 1, DIL_BAND, 2 * DIL_BAND, functools.partial(_dil_dist, dil=d))
              for _, d in DIL_PAIRS]

    h = x.reshape(n, D_MODEL)
    for layer in range(depth):
        qa, cva, ksw, ga, qkv_dilated = _in_projection(
            h, seq, attn_norm[layer], w_in[layer], nsa_q_norm[layer], nsa_k_norm[layer],
            dil_q_norm[layer], dil_k_norm[layer])
        kc, vc = _compress(cva, cmp_pos[layer], cmp_w1[layer], cmp_b1[layer], cmp_w2[layer],
                           nsa_k_norm[layer, 0])
        o_a = _nsa_attention(qa, kc, vc, ksw, ga, bias_c, bias_s, bias_w, batch, seq)
        dil_outs = [_dilated_attention(*qkv_dilated[i], bias_d[i], d) for i, (_, d) in enumerate(DIL_PAIRS)]
        moe_layer = layer % 2 == 1
        outs = _out_projection(o_a, dil_outs, h, out_norm[layer], w_out[layer], ffn_norm[layer],
                               router_w[layer // 2] if moe_layer else None)
        if moe_layer:
            h1, v, route = outs
            h = _moe(v, route, h1, exp_w_gate[layer // 2], exp_w_up[layer // 2], exp_w_down[layer // 2])
        else:
            h1, v = outs
            h = _dense_ffn(v, h1, ffn_w_gate[layer // 2], ffn_w_up[layer // 2], ffn_w_down[layer // 2])
    return h.reshape(batch, seq, D_MODEL)
```

```python
import functools
import math

import jax
import jax.numpy as jnp
import numpy as np
from jax import lax
from jax.experimental import pallas as pl
from jax.experimental.pallas import tpu as pltpu

F32 = jnp.float32
BF16 = jnp.bfloat16

D_MODEL = 1024
HEAD_DIM = 64
N_HEADS_A = 8
N_KV_A = 2
HPG_A = N_HEADS_A // N_KV_A
N_HEADS_B = 8
WIDTH_A = N_HEADS_A * HEAD_DIM
WIDTH_B = N_HEADS_B * HEAD_DIM
KV_A = N_KV_A * HEAD_DIM
CMP_LEN = 32
CMP_STRIDE = 16
CMP_HIDDEN = 256
SEL_BLOCK = 64
SEL_TOPK = 16
WIN_A = 512
DIL_PAIRS = ((128, 1), (512, 4), (2048, 16))
N_BUCKETS = 32
MAX_DISTANCE = 2048
N_EXPERTS = 8
TOP_K = 2
RMS_EPS = 1e-6
NEG = -1e30
SCALE = HEAD_DIM ** -0.5
LOG2E = math.log2(math.e)

LANES = 128
MXU_DIM = 256
VMEM_LIMIT = 56 * 1024 * 1024

TM_PROJ = 512
TQ_NSA = 256
TK_NSA = 512
TOEP = 128
DIL_BAND = 128
DIL_ROWS = 512
TM_FFN = 512
TF_FFN = 1408
TM_MOE = 1024
TF_MOE = 512
DISPATCH_TOKENS = 1024
COMBINE_TOKENS = 512
DMA_ISSUE_UNROLL = 8

COL_QA = 0
COL_CVA = 512
COL_KSW = 768
COL_QB = 1280
COL_KB = 1792
COL_VB = 2304
COL_GA = 2816
IN_COLS = 2944
PAIRED_HEADS_A = np.array([h + g * HPG_A for h in range(HPG_A) for g in range(N_KV_A)])


def _cparams(sem, vmem=VMEM_LIMIT):
    return pltpu.CompilerParams(dimension_semantics=sem, vmem_limit_bytes=vmem)


def _dot(a, b):
    return jnp.dot(a, b, preferred_element_type=F32)


def _dot_nt(a, b):
    return lax.dot_general(a, b, (((1,), (1,)), ((), ())), preferred_element_type=F32)


def _split_bf16(x):
    hi = x.astype(BF16)
    lo = (x - hi.astype(F32)).astype(BF16)
    return hi, lo


def _silu(x):
    half = 0.5 * x
    return half * (1.0 + jnp.tanh(half))


def _bucket_thresholds():
    d = np.arange(0, 4 * MAX_DISTANCE, dtype=np.int64)
    max_exact = N_BUCKETS // 2
    scaled = np.log(np.maximum(d, 1).astype(np.float32) / np.float32(max_exact)) / np.float32(
        math.log(MAX_DISTANCE / max_exact))
    large = np.minimum(max_exact + (scaled.astype(np.float32) * (N_BUCKETS - max_exact)).astype(np.int32),
                       N_BUCKETS - 1)
    bucket = np.where(d < max_exact, d, large)
    assert np.all(np.diff(bucket) >= 0)
    return [int(np.argmax(bucket >= b)) for b in range(N_BUCKETS)]


_THR = _bucket_thresholds()
FAR_DIST = _THR[N_BUCKETS - 1]


def _bias_table_kernel(tbl_ref, out_ref, *, n_heads, rows, cols, dist_valid):
    i = pl.program_id(0)
    a = lax.broadcasted_iota(jnp.int32, (rows, cols), 0)
    c = lax.broadcasted_iota(jnp.int32, (rows, cols), 1)
    d, valid = dist_valid(i, a, c)
    for h in range(n_heads):
        acc = jnp.full((rows, cols), tbl_ref[h, 0], F32)
        for b in range(1, N_BUCKETS):
            acc = jnp.where(d >= _THR[b], tbl_ref[h, b], acc)
        out_ref[h, 0] = jnp.where(valid, acc, NEG)


def _bias_table(tbl, n_tiles, rows, cols, dist_valid):
    n_heads = tbl.shape[0]
    return pl.pallas_call(
        functools.partial(_bias_table_kernel, n_heads=n_heads, rows=rows, cols=cols, dist_valid=dist_valid),
        grid=(n_tiles,),
        in_specs=[pl.BlockSpec(memory_space=pltpu.SMEM)],
        out_specs=pl.BlockSpec((n_heads, 1, rows, cols), lambda i: (0, i, 0, 0)),
        out_shape=jax.ShapeDtypeStruct((n_heads, n_tiles, rows, cols), F32),
        compiler_params=_cparams(("arbitrary",)),
        name="bias_table",
    )(tbl)


def _cmp_dist(i, a, c, *, n_cmp):
    d = i * TQ_NSA + a - (c * CMP_STRIDE + CMP_LEN - 1)
    return d, (d >= 0) & (c < n_cmp)


def _sel_dist(i, a, c):
    d = (i - 1) * TOEP + a - c
    return d, d >= 0


def _win_dist(i, a, c):
    d = (i - 1) * TOEP + a - c
    return d, (d >= 0) & (d < WIN_A)


def _dil_dist(i, a, c, *, dil):
    n = DIL_BAND + a - c
    return n * dil, (n >= 0) & (n <= DIL_BAND)


def _inproj_kernel(h_ref, gn_ref, w_ref, gain_ref, bd_ref,
                   qa_ref, cva_ref, ksw_ref, ga_ref, *rest):
    dil_refs, stage_ref = rest[:-1], rest[-1]
    tm = h_ref.shape[0]

    def emit_dilated(y, which):
        dil_refs[which][0, 0] = y.astype(BF16)
        n_chunks = y.shape[1] // LANES
        for c in range(n_chunks):
            stage_ref[c] = y[:, c * LANES:(c + 1) * LANES]
        for d, (_, r) in enumerate(DIL_PAIRS):
            if r == 1:
                continue
            ref = dil_refs[3 * d + which]
            for rho in range(r):
                for c in range(n_chunks):
                    ref[0, rho, :, c * LANES:(c + 1) * LANES] = stage_ref[
                        c, pl.ds(rho, tm // r, stride=r), :].astype(BF16)

    x = h_ref[...]
    u = (x * lax.rsqrt(jnp.mean(x * x, axis=-1, keepdims=True) + RMS_EPS) * gn_ref[...]).astype(BF16)

    def proj(c0, width):
        return _dot(u, w_ref[:, c0:c0 + width])

    def headnorm(acc, c0):
        outs = []
        for j in range(acc.shape[1] // MXU_DIM):
            a = acc[:, j * MXU_DIM:(j + 1) * MXU_DIM]
            sq_hi, sq_lo = _split_bf16(a * a)
            ms = _dot(sq_hi, bd_ref[...]) + _dot(sq_lo, bd_ref[...])
            g = gain_ref[:, c0 + j * MXU_DIM:c0 + (j + 1) * MXU_DIM]
            outs.append(a * lax.rsqrt(ms + RMS_EPS) * g)
        return outs[0] if len(outs) == 1 else jnp.concatenate(outs, axis=1)

    qa_ref[...] = headnorm(proj(COL_QA, WIDTH_A), COL_QA).astype(BF16)
    cva = proj(COL_CVA, 2 * KV_A)
    for c in range(2 * KV_A // LANES):
        stage_ref[c] = cva[:, c * LANES:(c + 1) * LANES]
    for l in range(CMP_STRIDE):
        for c in range(2 * KV_A // LANES):
            cva_ref[0, :, l * 2 * KV_A + c * LANES:l * 2 * KV_A + (c + 1) * LANES] = stage_ref[
                c, pl.ds(l, tm // CMP_STRIDE, stride=CMP_STRIDE), :]
    ksw = proj(COL_KSW, 4 * KV_A)
    ksw_ref[:, :2 * KV_A] = headnorm(ksw[:, :2 * KV_A], COL_KSW).astype(BF16)
    ksw_ref[:, 2 * KV_A:] = ksw[:, 2 * KV_A:].astype(BF16)
    emit_dilated(headnorm(proj(COL_QB, WIDTH_B), COL_QB), 0)
    emit_dilated(headnorm(proj(COL_KB, WIDTH_B), COL_KB), 1)
    emit_dilated(proj(COL_VB, WIDTH_B), 2)
    ga_ref[...] = jax.nn.sigmoid(proj(COL_GA, LANES))


def _in_projection(h, seq, attn_norm, w_in, nsa_q_norm, nsa_k_norm, dil_q_norm, dil_k_norm):
    n = h.shape[0]
    o = np.cumsum((0, WIDTH_A, KV_A, KV_A, KV_A, KV_A, KV_A, KV_A, N_HEADS_A * 3, WIDTH_B, WIDTH_B, WIDTH_B))
    seg = [w_in[:, o[i]:o[i + 1]] for i in range(11)]
    qa, kc, vc, ks, vs, kw, vw, ga, qb, kb, vb = seg
    qa = qa.reshape(D_MODEL, N_HEADS_A, HEAD_DIM)[:, PAIRED_HEADS_A, :].reshape(D_MODEL, WIDTH_A)
    pad = jnp.zeros((D_MODEL, IN_COLS - COL_GA - N_HEADS_A * 3), w_in.dtype)
    w = jnp.concatenate([qa, kc, vc, ks, kw, vs, vw, qb, kb, vb, ga, pad], axis=1).astype(BF16)
    ones = jnp.ones((IN_COLS,), F32)
    gain = ones
    gain = gain.at[COL_QA:COL_QA + WIDTH_A].set(jnp.tile(nsa_q_norm, N_HEADS_A) * (SCALE * LOG2E))
    gain = gain.at[COL_KSW:COL_KSW + KV_A].set(jnp.tile(nsa_k_norm[1], N_KV_A))
    gain = gain.at[COL_KSW + KV_A:COL_KSW + 2 * KV_A].set(jnp.tile(nsa_k_norm[2], N_KV_A))
    gain = gain.at[COL_QB:COL_QB + WIDTH_B].set(jnp.tile(dil_q_norm, N_HEADS_B) * (SCALE * LOG2E))
    gain = gain.at[COL_KB:COL_KB + WIDTH_B].set(jnp.tile(dil_k_norm, N_HEADS_B))
    blk = np.arange(MXU_DIM) // HEAD_DIM
    bd = jnp.asarray((blk[:, None] == blk[None, :]).astype(np.float32) / HEAD_DIM, BF16)

    tm = TM_PROJ
    tiles_per_seq = seq // tm
    row = lambda width: pl.BlockSpec((tm, width), lambda i: (i, 0))
    full = lambda a: pl.BlockSpec(a.shape, lambda i: (0,) * a.ndim)
    gn = attn_norm.reshape(1, D_MODEL)
    gain = gain.reshape(1, IN_COLS)
    dil_specs, dil_shapes = [], []
    for _, r in DIL_PAIRS:
        spec = pl.BlockSpec((1, r, tm // r, WIDTH_B), lambda i: (i // tiles_per_seq, 0, i % tiles_per_seq, 0))
        dil_specs += [spec] * 3
        dil_shapes += [jax.ShapeDtypeStruct((n // seq, r, seq // r, WIDTH_B), BF16)] * 3
    outs = pl.pallas_call(
        _inproj_kernel,
        grid=(n // tm,),
        in_specs=[row(D_MODEL), full(gn), full(w), full(gain), full(bd)],
        out_specs=[row(WIDTH_A),
                   pl.BlockSpec((1, tm // CMP_STRIDE, CMP_STRIDE * 2 * KV_A),
                                lambda i: (i // tiles_per_seq, i % tiles_per_seq, 0)),
                   row(4 * KV_A), row(LANES)] + dil_specs,
        out_shape=[jax.ShapeDtypeStruct((n, WIDTH_A), BF16),
                   jax.ShapeDtypeStruct((n // seq, seq // CMP_STRIDE, CMP_STRIDE * 2 * KV_A), F32),
                   jax.ShapeDtypeStruct((n, 4 * KV_A), BF16), jax.ShapeDtypeStruct((n, LANES), F32)] + dil_shapes,
        scratch_shapes=[pltpu.VMEM((WIDTH_B // LANES, tm, LANES), F32)],
        compiler_params=_cparams(("parallel",)),
        name="in_projection",
    )(h, gn, w, gain, bd)
    qa, cva, ksw, ga = outs[:4]
    qkv_dilated = [outs[4 + 3 * d:7 + 3 * d] for d in range(len(DIL_PAIRS))]
    return qa, cva, ksw, ga, qkv_dilated


def _gelu_tanh(x):
    return 0.5 * x * (1.0 + jnp.tanh(math.sqrt(2.0 / math.pi) * (x + 0.044715 * (x * x * x))))


def _compress_kernel(x_ref, pos_ref, w1_ref, b1_ref, w2_ref, kg_ref, kc_ref, vc_ref):
    rows = x_ref.shape[1]
    half = CMP_LEN // 2
    for which, out_ref in ((0, kc_ref), (1, vc_ref)):
        top = jnp.zeros((rows, 2 * CMP_HIDDEN), F32)
        bot = jnp.zeros((rows, 2 * CMP_HIDDEN), F32)
        for l in range(half):
            c0 = l * 2 * KV_A + which * KV_A
            a = x_ref[0, :, c0:c0 + KV_A]
            top += _dot((a + pos_ref[which, l:l + 1, :]).astype(BF16), w1_ref[which, l])
            bot += _dot((a + pos_ref[which, half + l:half + l + 1, :]).astype(BF16), w1_ref[which, half + l])
        hid = top + pltpu.roll(bot, rows - 1, axis=0) + b1_ref[which]
        y = _dot(_gelu_tanh(hid).astype(BF16), w2_ref[which])
        if which == 0:
            parts = []
            for g in range(N_KV_A):
                yg = y[:, g * HEAD_DIM:(g + 1) * HEAD_DIM]
                parts.append(yg * lax.rsqrt(jnp.mean(yg * yg, axis=-1, keepdims=True) + RMS_EPS))
            y = jnp.concatenate(parts, axis=1) * kg_ref[...]
        out_ref[0] = y.astype(BF16)


def _compress(x, cmp_pos, cmp_w1, cmp_b1, cmp_w2, k_norm0):
    batch, rows, _ = x.shape
    pos = jnp.tile(cmp_pos, (1, 1, N_KV_A))
    w1 = cmp_w1.reshape(2, CMP_LEN, HEAD_DIM, CMP_HIDDEN).astype(BF16)
    z1 = jnp.zeros_like(w1)
    w1 = jnp.concatenate([jnp.concatenate([w1, z1], axis=3), jnp.concatenate([z1, w1], axis=3)], axis=2)
    b1 = jnp.tile(cmp_b1, (1, N_KV_A)).reshape(2, 1, 2 * CMP_HIDDEN)
    w2 = cmp_w2.astype(BF16)
    z2 = jnp.zeros_like(w2)
    w2 = jnp.concatenate([jnp.concatenate([w2, z2], axis=2), jnp.concatenate([z2, w2], axis=2)], axis=1)
    kg = jnp.tile(k_norm0, N_KV_A).reshape(1, KV_A)
    full = lambda a: pl.BlockSpec(a.shape, lambda b: (0,) * a.ndim)
    out = pl.BlockSpec((1, rows, KV_A), lambda b: (b, 0, 0))
    return pl.pallas_call(
        _compress_kernel,
        grid=(batch,),
        in_specs=[pl.BlockSpec((1, rows, x.shape[2]), lambda b: (b, 0, 0)),
                  full(pos), full(w1), full(b1), full(w2), full(kg)],
        out_specs=[out, out],
        out_shape=[jax.ShapeDtypeStruct((batch, rows, KV_A), BF16)] * 2,
        compiler_params=_cparams(("parallel",)),
        name="nsa_compress",
    )(x, pos, w1, b1, w2, kg)


def _toeplitz_bias(tbl_ref, g, base, n_a, n_c):
    far = tbl_ref.shape[1] - 2
    rows = []
    for a in range(n_a):
        tiles = [tbl_ref[g * HPG_A:(g + 1) * HPG_A, jnp.clip(base + a - c, -1, far) + 1] for c in range(n_c)]
        rows.append(jnp.concatenate(tiles, axis=2))
    return jnp.concatenate(rows, axis=1)


def _nsa_kernel(q_ref, kc_ref, vc_ref, ksw_ref, ga_ref, bc_ref, bs_ref, bw_ref, ex_ref, gp_ref, o_ref,
                imp_ref, *, seq):
    tq, tk = TQ_NSA, TK_NSA
    qi = pl.program_id(1)
    rows_c = seq // CMP_STRIDE
    n_sb = seq // SEL_BLOCK
    k_sel = min(SEL_TOPK, n_sb)
    rows = HPG_A * tq
    groups = range(N_KV_A)

    lane = lax.broadcasted_iota(jnp.int32, (1, LANES), 1)
    low_half = lane < HEAD_DIM
    half_bf = [jnp.where(low_half, 1.0, 0.0).astype(BF16), jnp.where(low_half, 0.0, 1.0).astype(BF16)]
    qs = [jnp.concatenate([q_ref[:, c * LANES:(c + 1) * LANES] * half_bf[g] for c in range(HPG_A)], axis=0)
          for g in groups]

    def row_sums(p_bf):
        return _dot(p_bf, jnp.ones((p_bf.shape[1], LANES), BF16))

    jj = lax.broadcasted_iota(jnp.int32, (n_sb, rows_c), 0) * SEL_BLOCK
    nn = lax.broadcasted_iota(jnp.int32, (n_sb, rows_c), 1) * CMP_STRIDE
    ov = jnp.maximum(jnp.minimum(nn + CMP_LEN, jj + SEL_BLOCK) - jnp.maximum(nn, jj), 0)
    ov_t = (ov.astype(F32) * (1.0 / CMP_LEN)).astype(BF16)
    blk = lax.broadcasted_iota(jnp.int32, (n_sb, tq), 0)
    tpos = qi * tq + lax.broadcasted_iota(jnp.int32, (n_sb, tq), 1)
    cur = tpos // SEL_BLOCK
    forced = (blk == 0) | (blk == cur) | (blk == cur - 1)
    future = blk * SEL_BLOCK > tpos

    o_c, unsel_bf = [], []
    for g in groups:
        s = _dot_nt(qs[g], kc_ref[0]).reshape(HPG_A, tq, rows_c) + bc_ref[g * HPG_A:(g + 1) * HPG_A, 0]
        m = jnp.max(s, axis=-1, keepdims=True)
        e = jnp.where(s > 0.5 * NEG, jnp.exp2(s - m), 0.0)
        e_bf = e.reshape(rows, rows_c).astype(BF16)
        inv = 1.0 / jnp.maximum(row_sums(e_bf), 1e-30)
        o_c.append(_dot(e_bf, vc_ref[0]) * inv)
        inv4 = inv.reshape(HPG_A, tq, LANES)
        p = e * jnp.concatenate([inv4] * (rows_c // LANES), axis=2)
        p_sum = p[0] + p[1] + p[2] + p[3]
        p_hi, p_lo = _split_bf16(p_sum)
        p_lo2 = (p_sum - p_hi.astype(F32) - p_lo.astype(F32)).astype(BF16)
        imp = _dot_nt(ov_t, p_hi) + _dot_nt(ov_t, p_lo) + _dot_nt(ov_t, p_lo2)
        imp = jnp.where(forced, 1e6, jnp.where(future, -1e6, imp))
        imp_ref[...] = imp

        def count_beaten(i0, cnt):
            for u in range(tq // SEL_BLOCK):
                i = i0 * (tq // SEL_BLOCK) + u
                ri = jnp.broadcast_to(imp_ref[pl.ds(i, 1), :], (n_sb, tq))
                later = jnp.where(blk > i, 1.0, 0.0)
                cnt = cnt + jnp.where(ri > imp, 1.0, jnp.where(ri == imp, later, 0.0))
            return cnt

        cnt = lax.fori_loop(0, qi + 1, count_beaten, jnp.zeros((n_sb, tq), F32))
        unsel_bf.append(jnp.where(cnt < k_sel, 0.0, NEG).T.astype(BF16))

    def with_ones(v_pair, g):
        return v_pair * half_bf[g] + half_bf[1 - g]

    def normalised(acc):
        return acc / pltpu.roll(acc, HEAD_DIM, axis=1)

    def sel_body(kj, carry):
        r0 = pl.multiple_of(kj * tk, tk)
        k = ksw_ref[pl.ds(r0, tk), 0:KV_A]
        v = ksw_ref[pl.ds(r0, tk), 2 * KV_A:3 * KV_A]
        out = []
        for g in groups:
            m, acc = carry[g]
            s = _dot_nt(qs[g], k).reshape(HPG_A, tq, tk)
            s = s + _toeplitz_bias(bs_ref, g, (tq // TOEP) * qi - (tk // TOEP) * kj, tq // TOEP, tk // TOEP)
            madd = _dot(unsel_bf[g], ex_ref[kj])
            s = (s + madd[None]).reshape(rows, tk)
            m_new = jnp.maximum(m, jnp.max(s, axis=-1, keepdims=True))
            p = jnp.exp2(s - m_new).astype(BF16)
            out.append((m_new, jnp.exp2(m - m_new) * acc + _dot(p, with_ones(v, g))))
        return tuple(out)

    init = tuple((jnp.full((rows, 1), NEG, F32), jnp.zeros((rows, LANES), F32)) for _ in groups)
    n_tiles = (qi * tq + tq + tk - 1) // tk
    sel_out = lax.fori_loop(0, n_tiles, sel_body, init)
    o_s = [normalised(acc) for _, acc in sel_out]

    n_wk = WIN_A + tq
    start = pl.multiple_of(jnp.maximum(qi * tq - WIN_A, 0), tq)
    kw = ksw_ref[pl.ds(start, n_wk), KV_A:2 * KV_A]
    vw = ksw_ref[pl.ds(start, n_wk), 3 * KV_A:4 * KV_A]
    o_w = []
    for g in groups:
        s = _dot_nt(qs[g], kw).reshape(HPG_A, tq, n_wk)
        s = s + _toeplitz_bias(bw_ref, g, (qi * tq - start) // TOEP, tq // TOEP, n_wk // TOEP)
        s = s.reshape(rows, n_wk)
        p = jnp.exp2(s - jnp.max(s, axis=-1, keepdims=True)).astype(BF16)
        o_w.append(normalised(_dot(p, with_ones(vw, g))))

    g_hi, g_lo = _split_bf16(ga_ref[...])
    gates = _dot(g_hi, gp_ref[...]) + _dot(g_lo, gp_ref[...])
    for c in range(HPG_A):
        rs = slice(c * tq, (c + 1) * tq)
        out = jnp.zeros((tq, LANES), F32)
        for j, o in enumerate((o_c, o_s, o_w)):
            gate = gates[:, j * WIDTH_A + c * LANES:j * WIDTH_A + (c + 1) * LANES]
            out = out + gate * jnp.where(low_half, o[0][rs], o[1][rs])
        o_ref[:, c * LANES:(c + 1) * LANES] = out


def _nsa_attention(qa, kc, vc, ksw, ga, bias_c, bias_s, bias_w, batch, seq):
    tq, tk = TQ_NSA, TK_NSA
    nq = seq // tq
    n = batch * seq
    rows_c = seq // CMP_STRIDE
    n_sb = seq // SEL_BLOCK
    assert rows_c % LANES == 0 and seq % tk == 0 and seq >= WIN_A + tq and WIN_A % tq == 0
    key_blk = (np.arange(seq) // SEL_BLOCK).reshape(seq // tk, 1, tk)
    expand = jnp.asarray((key_blk == np.arange(n_sb).reshape(1, n_sb, 1)).astype(np.float32), BF16)
    gp = np.zeros((LANES, 3 * WIDTH_A), np.float32)
    for pos, head in enumerate(PAIRED_HEADS_A):
        for j in range(3):
            gp[3 * head + j, j * WIDTH_A + pos * HEAD_DIM:j * WIDTH_A + (pos + 1) * HEAD_DIM] = 1.0
    gp = jnp.asarray(gp, BF16)
    resident = lambda a: pl.BlockSpec(a.shape, lambda b, i: (0,) * a.ndim, pipeline_mode=pl.Buffered(1))
    return pl.pallas_call(
        functools.partial(_nsa_kernel, seq=seq),
        grid=(batch, nq),
        in_specs=[
            pl.BlockSpec((tq, WIDTH_A), lambda b, i: (b * nq + i, 0)),
            pl.BlockSpec((1, rows_c, KV_A), lambda b, i: (b, 0, 0)),
            pl.BlockSpec((1, rows_c, KV_A), lambda b, i: (b, 0, 0)),
            pl.BlockSpec((seq, 4 * KV_A), lambda b, i: (b, 0)),
            pl.BlockSpec((tq, LANES), lambda b, i: (b * nq + i, 0)),
            pl.BlockSpec((N_HEADS_A, 1, tq, rows_c), lambda b, i: (0, i, 0, 0)),
            resident(bias_s), resident(bias_w), resident(expand), resident(gp),
        ],
        out_specs=pl.BlockSpec((tq, WIDTH_A), lambda b, i: (b * nq + i, 0)),
        out_shape=jax.ShapeDtypeStruct((n, WIDTH_A), F32),
        scratch_shapes=[pltpu.VMEM((n_sb, tq), F32)],
        compiler_params=_cparams(("parallel", "arbitrary")),
        name="nsa_attention",
    )(qa, kc, vc, ksw, ga, bias_c, bias_s, bias_w, expand, gp)


def _dilated_kernel(q_ref, kp_ref, kc_ref, vp_ref, vc_ref, bias_ref, o_ref, lse_ref):
    band = DIL_BAND
    first = pl.program_id(2) == 0
    prev_mask = jnp.where(first, NEG, 0.0)
    lane = lax.broadcasted_iota(jnp.int32, (1, LANES), 1)
    low_half = lane < HEAD_DIM
    half_bf = [jnp.where(low_half, 1.0, 0.0).astype(BF16), jnp.where(low_half, 0.0, 1.0).astype(BF16)]
    den_sel = [jnp.broadcast_to(hm, (2 * band, LANES)) for hm in half_bf]
    key_col = lax.broadcasted_iota(jnp.int32, (1, 2 * band), 1)
    prev_cols = jnp.where(key_col < band, prev_mask, 0.0)
    n_cls, n_sub = q_ref.shape[0], q_ref.shape[1] // band
    for cl, sub in ((cl, sub) for cl in range(n_cls) for sub in range(n_sub)):
        r_cur = slice(sub * band, (sub + 1) * band)
        for c in range(WIDTH_B // LANES):
            cols = slice(c * LANES, (c + 1) * LANES)
            q = q_ref[cl, r_cur, cols]
            if sub == 0:
                k_cat = jnp.concatenate([kp_ref[cl, :, cols], kc_ref[cl, r_cur, cols]], axis=0)
                v_cat = jnp.concatenate([vp_ref[cl, :, cols], vc_ref[cl, r_cur, cols]], axis=0)
            else:
                k_cat = kc_ref[cl, (sub - 1) * band:(sub + 1) * band, cols]
                v_cat = vc_ref[cl, (sub - 1) * band:(sub + 1) * band, cols]
            acc = jnp.zeros((band, 2 * LANES), F32)
            ms = []
            for j in range(2):
                h = 2 * c + j
                s = _dot_nt(q * half_bf[j], k_cat) + bias_ref[h, 0]
                if sub == 0:
                    s = s + prev_cols
                m = jnp.max(s, axis=-1, keepdims=True)
                e = jnp.exp2(s - m).astype(BF16)
                acc = acc + _dot(e, jnp.concatenate([v_cat * half_bf[j], den_sel[j]], axis=1))
                ms.append(m)
            den = acc[:, LANES:]
            o_ref[cl, r_cur, cols] = acc[:, :LANES] / den
            lse_ref[cl, r_cur, cols] = jnp.where(low_half, ms[0], ms[1]) + jnp.log2(den)


def _dilated_attention(q, k, v, bias, dil):
    band = DIL_BAND
    batch, _, length, _ = q.shape
    rows = min(DIL_ROWS, length)
    n_cls = min(dil, DIL_ROWS // rows)
    sub = rows // band
    cur = pl.BlockSpec((None, n_cls, rows, WIDTH_B), lambda b, r, n: (b, r, n, 0))
    prev = pl.BlockSpec((None, n_cls, band, WIDTH_B), lambda b, r, n: (b, r, jnp.maximum(n * sub - 1, 0), 0))
    return pl.pallas_call(
        _dilated_kernel,
        grid=(batch, dil // n_cls, length // rows),
        in_specs=[cur, prev, cur, prev, cur,
                  pl.BlockSpec(bias.shape, lambda b, r, n: (0, 0, 0, 0))],
        out_specs=[cur, cur],
        out_shape=[jax.ShapeDtypeStruct((batch, dil, length, WIDTH_B), F32)] * 2,
        compiler_params=_cparams(("parallel", "parallel", "arbitrary")),
        name=f"dilated_attention_d{dil}",
    )(q, k, k, v, v, bias)


def _outproj_kernel(*refs, with_router):
    if with_router:
        (oa_ref, o1_ref, o2_ref, o3_ref, l1_ref, l2_ref, l3_ref, h_ref, gn_ref, w_ref, fg_ref, rw_ref,
         h_out_ref, v_ref, route_ref, stage_ref) = refs
    else:
        (oa_ref, o1_ref, o2_ref, o3_ref, l1_ref, l2_ref, l3_ref, h_ref, gn_ref, w_ref, fg_ref,
         h_out_ref, v_ref, stage_ref) = refs
    tm = h_ref.shape[0]

    def token_order(ref, slot):
        r = ref.shape[1]
        if r == 1:
            return ref[0, 0]
        n_chunks = ref.shape[3] // LANES
        for rho in range(r):
            for c in range(n_chunks):
                stage_ref[slot, c, pl.ds(rho, tm // r, stride=r), :] = ref[0, rho, :, c * LANES:(c + 1) * LANES]
        return jnp.concatenate([stage_ref[slot, c] for c in range(n_chunks)], axis=1)

    l1, l2, l3 = token_order(l1_ref, 0), token_order(l2_ref, 0), token_order(l3_ref, 1)
    m = jnp.maximum(jnp.maximum(l1, l2), l3)
    e1, e2, e3 = jnp.exp2(l1 - m), jnp.exp2(l2 - m), jnp.exp2(l3 - m)
    o1, o2, o3 = token_order(o1_ref, 0), token_order(o2_ref, 2), token_order(o3_ref, 3)
    ob = (e1 * o1 + e2 * o2 + e3 * o3) / (e1 + e2 + e3)

    def norm(x, g):
        return x * lax.rsqrt(jnp.mean(x * x, axis=-1, keepdims=True) + RMS_EPS) * g

    o = jnp.concatenate([norm(oa_ref[...], gn_ref[:, :WIDTH_A]), norm(ob, gn_ref[:, WIDTH_A:])], axis=1)
    h = h_ref[...] + _dot(o.astype(BF16), w_ref[...])
    h_out_ref[...] = h
    v = norm(h, fg_ref[...])
    v_ref[...] = v.astype(v_ref.dtype)
    if with_router:
        v_hi, v_lo = _split_bf16(v)
        w_hi, w_lo = _split_bf16(rw_ref[...])
        logits = _dot(v_hi, w_hi) + _dot(v_lo, w_hi) + _dot(v_hi, w_lo)
        lg = logits.T[:N_EXPERTS]
        row = lax.broadcasted_iota(jnp.int32, lg.shape, 0).astype(F32)
        m1 = jnp.max(lg, axis=0, keepdims=True)
        i1 = jnp.min(jnp.where(lg == m1, row, float(N_EXPERTS)), axis=0, keepdims=True)
        lg2 = jnp.where(row == i1, -jnp.inf, lg)
        m2 = jnp.max(lg2, axis=0, keepdims=True)
        i2 = jnp.min(jnp.where(lg2 == m2, row, float(N_EXPERTS)), axis=0, keepdims=True)
        e = jnp.exp(m2 - m1)
        g1 = 1.0 / (1.0 + e)
        g2 = e / (1.0 + e)
        route_ref[...] = jnp.where(row == 0, i1, jnp.where(
            row == 1, i2, jnp.where(row == 2, g1, jnp.where(row == 3, g2, 0.0))))


def _out_projection(o_a, dil_outs, h, out_norm, w_out, ffn_norm, router_w):
    n = h.shape[0]
    tm = TM_PROJ
    with_router = router_w is not None
    row = lambda width: pl.BlockSpec((tm, width), lambda i: (i, 0))
    full = lambda a: pl.BlockSpec(a.shape, lambda i: (0,) * a.ndim)
    (o1, l1), (o2, l2), (o3, l3) = dil_outs
    gn_a = out_norm[:WIDTH_A].reshape(N_HEADS_A, HEAD_DIM)[PAIRED_HEADS_A].reshape(WIDTH_A)
    gn = jnp.concatenate([gn_a, out_norm[WIDTH_A:]]).reshape(1, -1)
    fg = ffn_norm.reshape(1, -1)
    w_a = w_out[:WIDTH_A].reshape(N_HEADS_A, HEAD_DIM, D_MODEL)[PAIRED_HEADS_A].reshape(WIDTH_A, D_MODEL)
    w = jnp.concatenate([w_a, w_out[WIDTH_A:]], axis=0).astype(BF16)
    tiles_per_seq = o1.shape[2] // tm
    grouped = lambda a: pl.BlockSpec((1, a.shape[1], tm // a.shape[1], WIDTH_B),
                                     lambda i: (i // tiles_per_seq, 0, i % tiles_per_seq, 0))
    args = [o_a, o1, o2, o3, l1, l2, l3, h, gn, w, fg]
    in_specs = ([row(WIDTH_A)] + [grouped(a) for a in args[1:7]]
                + [row(D_MODEL), full(gn), full(w), full(fg)])
    out_specs = [row(D_MODEL), row(D_MODEL)]
    out_shape = [jax.ShapeDtypeStruct((n, D_MODEL), F32),
                 jax.ShapeDtypeStruct((n, D_MODEL), F32 if with_router else BF16)]
    if with_router:
        rw = jnp.pad(router_w, ((0, 0), (0, LANES - N_EXPERTS)))
        args.append(rw)
        in_specs.append(full(rw))
        out_specs.append(pl.BlockSpec((N_EXPERTS, tm), lambda i: (0, i)))
        out_shape.append(jax.ShapeDtypeStruct((N_EXPERTS, n), F32))
    return pl.pallas_call(
        functools.partial(_outproj_kernel, with_router=with_router),
        grid=(n // tm,),
        in_specs=in_specs,
        out_specs=out_specs,
        out_shape=out_shape,
        scratch_shapes=[pltpu.VMEM((4, WIDTH_B // LANES, tm, LANES), F32)],
        compiler_params=_cparams(("parallel",)),
        name="out_projection",
    )(*args)


def _ffn_kernel(v_ref, h_ref, wg_ref, wu_ref, wd_ref, o_ref, acc_ref):
    f = pl.program_id(1)

    @pl.when(f == 0)
    def _():
        acc_ref[...] = jnp.zeros_like(acc_ref)

    v = v_ref[...]
    gate = _dot(v, wg_ref[...])
    up = _dot(v, wu_ref[...])
    acc_ref[...] += _dot((_silu(gate) * up).astype(BF16), wd_ref[...])

    @pl.when(f == pl.num_programs(1) - 1)
    def _():
        o_ref[...] = h_ref[...] + acc_ref[...]


def _dense_ffn(v, h, w_gate, w_up, w_down):
    n = h.shape[0]
    d_ff = w_gate.shape[1]
    tm, tf = TM_FFN, TF_FFN
    return pl.pallas_call(
        _ffn_kernel,
        grid=(n // tm, d_ff // tf),
        in_specs=[pl.BlockSpec((tm, D_MODEL), lambda i, f: (i, 0)),
                  pl.BlockSpec((tm, D_MODEL), lambda i, f: (i, 0)),
                  pl.BlockSpec((D_MODEL, tf), lambda i, f: (0, f)),
                  pl.BlockSpec((D_MODEL, tf), lambda i, f: (0, f)),
                  pl.BlockSpec((tf, D_MODEL), lambda i, f: (f, 0))],
        out_specs=pl.BlockSpec((tm, D_MODEL), lambda i, f: (i, 0)),
        out_shape=jax.ShapeDtypeStruct((n, D_MODEL), F32),
        scratch_shapes=[pltpu.VMEM((tm, D_MODEL), F32)],
        compiler_params=_cparams(("parallel", "arbitrary")),
        name="dense_ffn",
    )(v, h, w_gate.astype(BF16), w_up.astype(BF16), w_down.astype(BF16))


def _row_copy(src_ref, src_row, dst_ref, dst_row, sem):
    return pltpu.make_async_copy(src_ref.at[pl.ds(src_row, 1)], dst_ref.at[pl.ds(dst_row, 1)], sem)


def _dispatch_kernel(dest_ref, v_ref, xs_in_ref, xs_ref, sem):
    del xs_in_ref
    n_slots = dest_ref.shape[2]

    def issue(i, carry):
        for k in range(TOP_K):
            _row_copy(v_ref, i, xs_ref, dest_ref[0, 0, TOP_K * i + k], sem).start()
        return carry

    lax.fori_loop(0, n_slots // TOP_K, issue, 0, unroll=DMA_ISSUE_UNROLL)
    pltpu.make_async_copy(xs_ref.at[pl.ds(0, n_slots)], xs_ref.at[pl.ds(0, n_slots)], sem).wait()


def _dispatch(v, dest, n_rows):
    n = v.shape[0]
    tokens = DISPATCH_TOKENS
    slots = tokens * TOP_K
    return pl.pallas_call(
        _dispatch_kernel,
        grid=(n // tokens,),
        in_specs=[pl.BlockSpec((1, 1, slots), lambda i: (i, 0, 0), memory_space=pltpu.SMEM),
                  pl.BlockSpec((tokens, D_MODEL), lambda i: (i, 0)),
                  pl.BlockSpec(memory_space=pl.ANY)],
        out_specs=pl.BlockSpec(memory_space=pl.ANY),
        out_shape=jax.ShapeDtypeStruct((n_rows, D_MODEL), F32),
        scratch_shapes=[pltpu.SemaphoreType.DMA(())],
        input_output_aliases={2: 0},
        compiler_params=_cparams(("arbitrary",)),
        name="moe_dispatch",
    )(dest.reshape(n // tokens, 1, slots), v, jnp.zeros((n_rows, D_MODEL), F32))


def _gmm_kernel(te_ref, tv_ref, x_ref, wg_ref, wu_ref, wd_ref, y_ref, xb_ref, acc_ref):
    c = pl.program_id(0)
    f = pl.program_id(1)
    valid = tv_ref[c] > 0

    @pl.when(f == 0)
    def _():
        acc_ref[...] = jnp.zeros_like(acc_ref)
        xb_ref[...] = x_ref[...].astype(BF16)

    @pl.when(valid)
    def _():
        x = xb_ref[...]
        gate = _dot(x, wg_ref[0])
        up = _dot(x, wu_ref[0])
        acc_ref[...] += _dot((_silu(gate) * up).astype(BF16), wd_ref[0])

    @pl.when(f == pl.num_programs(1) - 1)
    def _():
        y_ref[...] = acc_ref[...]


def _grouped_swiglu(xs, tile_expert, tile_valid, w_gate, w_up, w_down):
    n_rows = xs.shape[0]
    d_ff = w_gate.shape[2]
    tm, tf = TM_MOE, TF_MOE
    grid_spec = pltpu.PrefetchScalarGridSpec(
        num_scalar_prefetch=2,
        grid=(n_rows // tm, d_ff // tf),
        in_specs=[pl.BlockSpec((tm, D_MODEL), lambda c, f, te, tv: (c, 0)),
                  pl.BlockSpec((1, D_MODEL, tf), lambda c, f, te, tv: (te[c], 0, f)),
                  pl.BlockSpec((1, D_MODEL, tf), lambda c, f, te, tv: (te[c], 0, f)),
                  pl.BlockSpec((1, tf, D_MODEL), lambda c, f, te, tv: (te[c], f, 0))],
        out_specs=pl.BlockSpec((tm, D_MODEL), lambda c, f, te, tv: (c, 0)),
        scratch_shapes=[pltpu.VMEM((tm, D_MODEL), BF16), pltpu.VMEM((tm, D_MODEL), F32)],
    )
    return pl.pallas_call(
        _gmm_kernel,
        grid_spec=grid_spec,
        out_shape=jax.ShapeDtypeStruct((n_rows, D_MODEL), F32),
        compiler_params=_cparams(("parallel", "arbitrary")),
        name="moe_grouped_swiglu",
    )(tile_expert, tile_valid, xs, w_gate.astype(BF16), w_up.astype(BF16), w_down.astype(BF16))


def _combine_kernel(dest_ref, dest_next_ref, y_ref, gate_ref, h_ref, o_ref, buf_ref, sem):
    tc = h_ref.shape[0]
    step = pl.program_id(0)
    slot = step % 2

    def start_gather(idx_ref, into):
        def issue(i, carry):
            for k in range(TOP_K):
                _row_copy(y_ref, idx_ref[0, 0, TOP_K * i + k], buf_ref.at[into, k], i, sem.at[into]).start()
            return carry

        lax.fori_loop(0, tc, issue, 0, unroll=DMA_ISSUE_UNROLL)

    @pl.when(step == 0)
    def _():
        start_gather(dest_ref, 0)

    @pl.when(step + 1 < pl.num_programs(0))
    def _():
        start_gather(dest_next_ref, 1 - slot)

    for k in range(TOP_K):
        pltpu.make_async_copy(y_ref.at[pl.ds(0, tc)], buf_ref.at[slot, k], sem.at[slot]).wait()
    g = gate_ref[...]
    o_ref[...] = h_ref[...] + g[:, 0:1] * buf_ref[slot, 0] + g[:, 1:2] * buf_ref[slot, 1]


def _combine(y, dest, gates, h):
    n = h.shape[0]
    tc = COMBINE_TOKENS
    steps = n // tc
    dest = dest.reshape(steps, 1, tc * TOP_K)
    return pl.pallas_call(
        _combine_kernel,
        grid=(steps,),
        in_specs=[pl.BlockSpec((1, 1, tc * TOP_K), lambda i: (i, 0, 0), memory_space=pltpu.SMEM),
                  pl.BlockSpec((1, 1, tc * TOP_K), lambda i: (jnp.minimum(i + 1, steps - 1), 0, 0),
                               memory_space=pltpu.SMEM),
                  pl.BlockSpec(memory_space=pl.ANY),
                  pl.BlockSpec((tc, TOP_K), lambda i: (i, 0)),
                  pl.BlockSpec((tc, D_MODEL), lambda i: (i, 0))],
        out_specs=pl.BlockSpec((tc, D_MODEL), lambda i: (i, 0)),
        out_shape=jax.ShapeDtypeStruct((n, D_MODEL), F32),
        scratch_shapes=[pltpu.VMEM((2, TOP_K, tc, D_MODEL), F32), pltpu.SemaphoreType.DMA((2,))],
        compiler_params=_cparams(("arbitrary",)),
        name="moe_combine",
    )(dest, dest, y, gates, h)


def _moe(v, route, h, w_gate, w_up, w_down):
    n = h.shape[0]
    nk = n * TOP_K
    tm = TM_MOE
    e_flat = route[:TOP_K].T.astype(jnp.int32).reshape(nk)
    gates = route[TOP_K:2 * TOP_K].T
    onehot = (e_flat[:, None] == jnp.arange(N_EXPERTS)[None, :]).astype(jnp.int32)
    csum = jnp.cumsum(onehot, axis=0)
    counts = csum[-1]
    rank = jnp.sum(jnp.where(onehot > 0, csum, 0), axis=1) - 1
    padded = (counts + tm - 1) // tm * tm
    pend = jnp.cumsum(padded)
    pstart = pend - padded
    dest = (jnp.sum(jnp.where(onehot > 0, pstart[None, :], 0), axis=1) + rank).astype(jnp.int32)
    n_tiles = nk // tm + N_EXPERTS
    tile_start = jnp.arange(n_tiles) * tm
    tile_expert = jnp.minimum(jnp.sum(tile_start[:, None] >= pend[None, :], axis=1), N_EXPERTS - 1)
    tile_valid = (tile_start < pend[-1]).astype(jnp.int32)
    xs = _dispatch(v, dest, n_tiles * tm)
    y = _grouped_swiglu(xs, tile_expert.astype(jnp.int32), tile_valid, w_gate, w_up, w_down)
    return _combine(y, dest, gates, h)


def kernel(x, rel_bias, attn_norm, w_in, nsa_q_norm, nsa_k_norm, cmp_pos, cmp_w1, cmp_b1, cmp_w2,
           dil_q_norm, dil_k_norm, out_norm, w_out, ffn_norm, ffn_w_gate, ffn_w_up, ffn_w_down,
           router_w, exp_w_gate, exp_w_up, exp_w_down):
    batch, seq, _ = x.shape
    depth = attn_norm.shape[0]
    n = batch * seq
    assert all((seq // d) % min(DIL_ROWS, seq // d) == 0 and (seq // d) % DIL_BAND == 0 for _, d in DIL_PAIRS)
    assert seq % TM_PROJ == 0 and n % DISPATCH_TOKENS == 0
    assert all(w == d * DIL_BAND for w, d in DIL_PAIRS)

    tbl = rel_bias.astype(F32).T
    tbl = tbl * LOG2E
    tbl_a, tbl_b = tbl[:N_HEADS_A], tbl[N_HEADS_A:]
    max_off = seq // TOEP - 1
    far_sel = min(max_off, -(-(FAR_DIST + TOEP - 1) // TOEP))
    far_win = min(max_off, -(-(WIN_A + TOEP - 1) // TOEP))
    bias_c = _bias_table(tbl_a, seq // TQ_NSA, TQ_NSA, seq // CMP_STRIDE,
                         functools.partial(_cmp_dist, n_cmp=(seq - CMP_LEN) // CMP_STRIDE + 1))
    bias_s = _bias_table(tbl_a, far_sel + 2, TOEP, TOEP, _sel_dist)
    bias_w = _bias_table(tbl_a, far_win + 2, TOEP, TOEP, _win_dist)
    bias_d = [_bias_table(tbl_b, 1, DIL_BAND, 2 * DIL_BAND, functools.partial(_dil_dist, dil=d))
              for _, d in DIL_PAIRS]

    h = x.reshape(n, D_MODEL)
    for layer in range(depth):
        qa, cva, ksw, ga, qkv_dilated = _in_projection(
            h, seq, attn_norm[layer], w_in[layer], nsa_q_norm[layer], nsa_k_norm[layer],
            dil_q_norm[layer], dil_k_norm[layer])
        kc, vc = _compress(cva, cmp_pos[layer], cmp_w1[layer], cmp_b1[layer], cmp_w2[layer],
                           nsa_k_norm[layer, 0])
        o_a = _nsa_attention(qa, kc, vc, ksw, ga, bias_c, bias_s, bias_w, batch, seq)
        dil_outs = [_dilated_attention(*qkv_dilated[i], bias_d[i], d) for i, (_, d) in enumerate(DIL_PAIRS)]
        moe_layer = layer % 2 == 1
        outs = _out_projection(o_a, dil_outs, h, out_norm[layer], w_out[layer], ffn_norm[layer],
                               router_w[layer // 2] if moe_layer else None)
        if moe_layer:
            h1, v, route = outs
            h = _moe(v, route, h1, exp_w_gate[layer // 2], exp_w_up[layer // 2], exp_w_down[layer // 2])
        else:
            h1, v = outs
            h = _dense_ffn(v, h1, ffn_w_gate[layer // 2], ffn_w_up[layer // 2], ffn_w_down[layer // 2])
    return h.reshape(batch, seq, D_MODEL)
```

```python
import functools
import math

import jax
import jax.numpy as jnp
import numpy as np
from jax import lax
from jax.experimental import pallas as pl
from jax.experimental.pallas import tpu as pltpu

F32 = jnp.float32
BF16 = jnp.bfloat16

D_MODEL = 1024
HEAD_DIM = 64
N_HEADS_A = 8
N_KV_A = 2
HPG_A = N_HEADS_A // N_KV_A
N_HEADS_B = 8
WIDTH_A = N_HEADS_A * HEAD_DIM
WIDTH_B = N_HEADS_B * HEAD_DIM
KV_A = N_KV_A * HEAD_DIM
CMP_LEN = 32
CMP_STRIDE = 16
CMP_HIDDEN = 256
SEL_BLOCK = 64
SEL_TOPK = 16
WIN_A = 512
DIL_PAIRS = ((128, 1), (512, 4), (2048, 16))
N_BUCKETS = 32
MAX_DISTANCE = 2048
N_EXPERTS = 8
TOP_K = 2
RMS_EPS = 1e-6
NEG = -1e30
SCALE = HEAD_DIM ** -0.5
LOG2E = math.log2(math.e)

LANES = 128
MXU_DIM = 256
VMEM_LIMIT = 56 * 1024 * 1024

TM_PROJ = 512
TQ_NSA = 256
TK_NSA = 512
TOEP = 128
DIL_BAND = 128
DIL_ROWS = 512
TM_FFN = 512
TF_FFN = 1408
TM_MOE = 1024
TF_MOE = 512
DISPATCH_TOKENS = 1024
COMBINE_TOKENS = 512
DMA_ISSUE_UNROLL = 8

COL_QA = 0
COL_CVA = 512
COL_KSW = 768
COL_QB = 1280
COL_KB = 1792
COL_VB = 2304
COL_GA = 2816
IN_COLS = 2944
PAIRED_HEADS_A = np.array([h + g * HPG_A for h in range(HPG_A) for g in range(N_KV_A)])


def _cparams(sem, vmem=VMEM_LIMIT):
    return pltpu.CompilerParams(dimension_semantics=sem, vmem_limit_bytes=vmem)


def _dot(a, b):
    return jnp.dot(a, b, preferred_element_type=F32)


def _dot_nt(a, b):
    return lax.dot_general(a, b, (((1,), (1,)), ((), ())), preferred_element_type=F32)


def _split_bf16(x):
    hi = x.astype(BF16)
    lo = (x - hi.astype(F32)).astype(BF16)
    return hi, lo


def _silu(x):
    half = 0.5 * x
    return half * (1.0 + jnp.tanh(half))


def _bucket_thresholds():
    d = np.arange(0, 4 * MAX_DISTANCE, dtype=np.int64)
    max_exact = N_BUCKETS // 2
    scaled = np.log(np.maximum(d, 1).astype(np.float32) / np.float32(max_exact)) / np.float32(
        math.log(MAX_DISTANCE / max_exact))
    large = np.minimum(max_exact + (scaled.astype(np.float32) * (N_BUCKETS - max_exact)).astype(np.int32),
                       N_BUCKETS - 1)
    bucket = np.where(d < max_exact, d, large)
    assert np.all(np.diff(bucket) >= 0)
    return [int(np.argmax(bucket >= b)) for b in range(N_BUCKETS)]


_THR = _bucket_thresholds()
FAR_DIST = _THR[N_BUCKETS - 1]


def _bias_table_kernel(tbl_ref, out_ref, *, n_heads, rows, cols, dist_valid):
    i = pl.program_id(0)
    a = lax.broadcasted_iota(jnp.int32, (rows, cols), 0)
    c = lax.broadcasted_iota(jnp.int32, (rows, cols), 1)
    d, valid = dist_valid(i, a, c)
    for h in range(n_heads):
        acc = jnp.full((rows, cols), tbl_ref[h, 0], F32)
        for b in range(1, N_BUCKETS):
            acc = jnp.where(d >= _THR[b], tbl_ref[h, b], acc)
        out_ref[h, 0] = jnp.where(valid, acc, NEG)


def _bias_table(tbl, n_tiles, rows, cols, dist_valid):
    n_heads = tbl.shape[0]
    return pl.pallas_call(
        functools.partial(_bias_table_kernel, n_heads=n_heads, rows=rows, cols=cols, dist_valid=dist_valid),
        grid=(n_tiles,),
        in_specs=[pl.BlockSpec(memory_space=pltpu.SMEM)],
        out_specs=pl.BlockSpec((n_heads, 1, rows, cols), lambda i: (0, i, 0, 0)),
        out_shape=jax.ShapeDtypeStruct((n_heads, n_tiles, rows, cols), F32),
        compiler_params=_cparams(("arbitrary",)),
        name="bias_table",
    )(tbl)


def _cmp_dist(i, a, c, *, n_cmp):
    d = i * TQ_NSA + a - (c * CMP_STRIDE + CMP_LEN - 1)
    return d, (d >= 0) & (c < n_cmp)


def _sel_dist(i, a, c):
    d = (i - 1) * TOEP + a - c
    return d, d >= 0


def _win_dist(i, a, c):
    d = (i - 1) * TOEP + a - c
    return d, (d >= 0) & (d < WIN_A)


def _dil_dist(i, a, c, *, dil):
    n = DIL_BAND + a - c
    return n * dil, (n >= 0) & (n <= DIL_BAND)


def _inproj_kernel(h_ref, gn_ref, w_ref, gain_ref, bd_ref,
                   qa_ref, cva_ref, ksw_ref, ga_ref, *rest):
    dil_refs, stage_ref = rest[:-1], rest[-1]
    tm = h_ref.shape[0]

    def emit_dilated(y, which):
        dil_refs[which][0, 0] = y.astype(BF16)
        n_chunks = y.shape[1] // LANES
        for c in range(n_chunks):
            stage_ref[c] = y[:, c * LANES:(c + 1) * LANES]
        for d, (_, r) in enumerate(DIL_PAIRS):
            if r == 1:
                continue
            ref = dil_refs[3 * d + which]
            for rho in range(r):
                for c in range(n_chunks):
                    ref[0, rho, :, c * LANES:(c + 1) * LANES] = stage_ref[
                        c, pl.ds(rho, tm // r, stride=r), :].astype(BF16)

    x = h_ref[...]
    u = (x * lax.rsqrt(jnp.mean(x * x, axis=-1, keepdims=True) + RMS_EPS) * gn_ref[...]).astype(BF16)

    def proj(c0, width):
        return _dot(u, w_ref[:, c0:c0 + width])

    def headnorm(acc, c0):
        outs = []
        for j in range(acc.shape[1] // MXU_DIM):
            a = acc[:, j * MXU_DIM:(j + 1) * MXU_DIM]
            ms = _dot((a * a).astype(BF16), bd_ref[...])
            g = gain_ref[:, c0 + j * MXU_DIM:c0 + (j + 1) * MXU_DIM]
            outs.append(a * lax.rsqrt(ms + RMS_EPS) * g)
        return outs[0] if len(outs) == 1 else jnp.concatenate(outs, axis=1)

    qa_ref[...] = headnorm(proj(COL_QA, WIDTH_A), COL_QA).astype(BF16)
    cva = proj(COL_CVA, 2 * KV_A)
    for c in range(2 * KV_A // LANES):
        stage_ref[c] = cva[:, c * LANES:(c + 1) * LANES]
    for l in range(CMP_STRIDE):
        for c in range(2 * KV_A // LANES):
            cva_ref[0, :, l * 2 * KV_A + c * LANES:l * 2 * KV_A + (c + 1) * LANES] = stage_ref[
                c, pl.ds(l, tm // CMP_STRIDE, stride=CMP_STRIDE), :]
    ksw = proj(COL_KSW, 4 * KV_A)
    ksw_ref[:, :2 * KV_A] = headnorm(ksw[:, :2 * KV_A], COL_KSW).astype(BF16)
    ksw_ref[:, 2 * KV_A:] = ksw[:, 2 * KV_A:].astype(BF16)
    emit_dilated(headnorm(proj(COL_QB, WIDTH_B), COL_QB), 0)
    emit_dilated(headnorm(proj(COL_KB, WIDTH_B), COL_KB), 1)
    emit_dilated(proj(COL_VB, WIDTH_B), 2)
    ga_ref[...] = jax.nn.sigmoid(proj(COL_GA, LANES))


def _in_projection(h, seq, attn_norm, w_in, nsa_q_norm, nsa_k_norm, dil_q_norm, dil_k_norm):
    n = h.shape[0]
    o = np.cumsum((0, WIDTH_A, KV_A, KV_A, KV_A, KV_A, KV_A, KV_A, N_HEADS_A * 3, WIDTH_B, WIDTH_B, WIDTH_B))
    seg = [w_in[:, o[i]:o[i + 1]] for i in range(11)]
    qa, kc, vc, ks, vs, kw, vw, ga, qb, kb, vb = seg
    qa = qa.reshape(D_MODEL, N_HEADS_A, HEAD_DIM)[:, PAIRED_HEADS_A, :].reshape(D_MODEL, WIDTH_A)
    pad = jnp.zeros((D_MODEL, IN_COLS - COL_GA - N_HEADS_A * 3), w_in.dtype)
    w = jnp.concatenate([qa, kc, vc, ks, kw, vs, vw, qb, kb, vb, ga, pad], axis=1).astype(BF16)
    ones = jnp.ones((IN_COLS,), F32)
    gain = ones
    gain = gain.at[COL_QA:COL_QA + WIDTH_A].set(jnp.tile(nsa_q_norm, N_HEADS_A) * (SCALE * LOG2E))
    gain = gain.at[COL_KSW:COL_KSW + KV_A].set(jnp.tile(nsa_k_norm[1], N_KV_A))
    gain = gain.at[COL_KSW + KV_A:COL_KSW + 2 * KV_A].set(jnp.tile(nsa_k_norm[2], N_KV_A))
    gain = gain.at[COL_QB:COL_QB + WIDTH_B].set(jnp.tile(dil_q_norm, N_HEADS_B) * (SCALE * LOG2E))
    gain = gain.at[COL_KB:COL_KB + WIDTH_B].set(jnp.tile(dil_k_norm, N_HEADS_B))
    blk = np.arange(MXU_DIM) // HEAD_DIM
    bd = jnp.asarray((blk[:, None] == blk[None, :]).astype(np.float32) / HEAD_DIM, BF16)

    tm = TM_PROJ
    tiles_per_seq = seq // tm
    row = lambda width: pl.BlockSpec((tm, width), lambda i: (i, 0))
    full = lambda a: pl.BlockSpec(a.shape, lambda i: (0,) * a.ndim)
    gn = attn_norm.reshape(1, D_MODEL)
    gain = gain.reshape(1, IN_COLS)
    dil_specs, dil_shapes = [], []
    for _, r in DIL_PAIRS:
        spec = pl.BlockSpec((1, r, tm // r, WIDTH_B), lambda i: (i // tiles_per_seq, 0, i % tiles_per_seq, 0))
        dil_specs += [spec] * 3
        dil_shapes += [jax.ShapeDtypeStruct((n // seq, r, seq // r, WIDTH_B), BF16)] * 3
    outs = pl.pallas_call(
        _inproj_kernel,
        grid=(n // tm,),
        in_specs=[row(D_MODEL), full(gn), full(w), full(gain), full(bd)],
        out_specs=[row(WIDTH_A),
                   pl.BlockSpec((1, tm // CMP_STRIDE, CMP_STRIDE * 2 * KV_A),
                                lambda i: (i // tiles_per_seq, i % tiles_per_seq, 0)),
                   row(4 * KV_A), row(LANES)] + dil_specs,
        out_shape=[jax.ShapeDtypeStruct((n, WIDTH_A), BF16),
                   jax.ShapeDtypeStruct((n // seq, seq // CMP_STRIDE, CMP_STRIDE * 2 * KV_A), F32),
                   jax.ShapeDtypeStruct((n, 4 * KV_A), BF16), jax.ShapeDtypeStruct((n, LANES), F32)] + dil_shapes,
        scratch_shapes=[pltpu.VMEM((WIDTH_B // LANES, tm, LANES), F32)],
        compiler_params=_cparams(("parallel",)),
        name="in_projection",
    )(h, gn, w, gain, bd)
    qa, cva, ksw, ga = outs[:4]
    qkv_dilated = [outs[4 + 3 * d:7 + 3 * d] for d in range(len(DIL_PAIRS))]
    return qa, cva, ksw, ga, qkv_dilated


def _gelu_tanh(x):
    return 0.5 * x * (1.0 + jnp.tanh(math.sqrt(2.0 / math.pi) * (x + 0.044715 * (x * x * x))))


def _compress_kernel(x_ref, pos_ref, w1_ref, b1_ref, w2_ref, kg_ref, kc_ref, vc_ref):
    rows = x_ref.shape[1]
    half = CMP_LEN // 2
    for which, out_ref in ((0, kc_ref), (1, vc_ref)):
        top = jnp.zeros((rows, 2 * CMP_HIDDEN), F32)
        bot = jnp.zeros((rows, 2 * CMP_HIDDEN), F32)
        for l in range(half):
            c0 = l * 2 * KV_A + which * KV_A
            a = x_ref[0, :, c0:c0 + KV_A]
            top += _dot((a + pos_ref[which, l:l + 1, :]).astype(BF16), w1_ref[which, l])
            bot += _dot((a + pos_ref[which, half + l:half + l + 1, :]).astype(BF16), w1_ref[which, half + l])
        hid = top + pltpu.roll(bot, rows - 1, axis=0) + b1_ref[which]
        y = _dot(_gelu_tanh(hid).astype(BF16), w2_ref[which])
        if which == 0:
            parts = []
            for g in range(N_KV_A):
                yg = y[:, g * HEAD_DIM:(g + 1) * HEAD_DIM]
                parts.append(yg * lax.rsqrt(jnp.mean(yg * yg, axis=-1, keepdims=True) + RMS_EPS))
            y = jnp.concatenate(parts, axis=1) * kg_ref[...]
        out_ref[0] = y.astype(BF16)


def _compress(x, cmp_pos, cmp_w1, cmp_b1, cmp_w2, k_norm0):
    batch, rows, _ = x.shape
    pos = jnp.tile(cmp_pos, (1, 1, N_KV_A))
    w1 = cmp_w1.reshape(2, CMP_LEN, HEAD_DIM, CMP_HIDDEN).astype(BF16)
    z1 = jnp.zeros_like(w1)
    w1 = jnp.concatenate([jnp.concatenate([w1, z1], axis=3), jnp.concatenate([z1, w1], axis=3)], axis=2)
    b1 = jnp.tile(cmp_b1, (1, N_KV_A)).reshape(2, 1, 2 * CMP_HIDDEN)
    w2 = cmp_w2.astype(BF16)
    z2 = jnp.zeros_like(w2)
    w2 = jnp.concatenate([jnp.concatenate([w2, z2], axis=2), jnp.concatenate([z2, w2], axis=2)], axis=1)
    kg = jnp.tile(k_norm0, N_KV_A).reshape(1, KV_A)
    full = lambda a: pl.BlockSpec(a.shape, lambda b: (0,) * a.ndim)
    out = pl.BlockSpec((1, rows, KV_A), lambda b: (b, 0, 0))
    return pl.pallas_call(
        _compress_kernel,
        grid=(batch,),
        in_specs=[pl.BlockSpec((1, rows, x.shape[2]), lambda b: (b, 0, 0)),
                  full(pos), full(w1), full(b1), full(w2), full(kg)],
        out_specs=[out, out],
        out_shape=[jax.ShapeDtypeStruct((batch, rows, KV_A), BF16)] * 2,
        compiler_params=_cparams(("parallel",)),
        name="nsa_compress",
    )(x, pos, w1, b1, w2, kg)


def _toeplitz_bias(tbl_ref, g, base, n_a, n_c):
    far = tbl_ref.shape[1] - 2
    rows = []
    for a in range(n_a):
        tiles = [tbl_ref[g * HPG_A:(g + 1) * HPG_A, jnp.clip(base + a - c, -1, far) + 1] for c in range(n_c)]
        rows.append(jnp.concatenate(tiles, axis=2))
    return jnp.concatenate(rows, axis=1)


def _nsa_kernel(q_ref, kc_ref, vc_ref, ksw_ref, ga_ref, bc_ref, bs_ref, bw_ref, ex_ref, gp_ref, o_ref,
                imp_ref, *, seq):
    tq, tk = TQ_NSA, TK_NSA
    qi = pl.program_id(1)
    rows_c = seq // CMP_STRIDE
    n_sb = seq // SEL_BLOCK
    k_sel = min(SEL_TOPK, n_sb)
    rows = HPG_A * tq
    groups = range(N_KV_A)

    lane = lax.broadcasted_iota(jnp.int32, (1, LANES), 1)
    low_half = lane < HEAD_DIM
    half_bf = [jnp.where(low_half, 1.0, 0.0).astype(BF16), jnp.where(low_half, 0.0, 1.0).astype(BF16)]
    qs = [jnp.concatenate([q_ref[:, c * LANES:(c + 1) * LANES] * half_bf[g] for c in range(HPG_A)], axis=0)
          for g in groups]

    def row_sums(p_bf):
        return _dot(p_bf, jnp.ones((p_bf.shape[1], LANES), BF16))

    jj = lax.broadcasted_iota(jnp.int32, (n_sb, rows_c), 0) * SEL_BLOCK
    nn = lax.broadcasted_iota(jnp.int32, (n_sb, rows_c), 1) * CMP_STRIDE
    ov = jnp.maximum(jnp.minimum(nn + CMP_LEN, jj + SEL_BLOCK) - jnp.maximum(nn, jj), 0)
    ov_t = (ov.astype(F32) * (1.0 / CMP_LEN)).astype(BF16)
    blk = lax.broadcasted_iota(jnp.int32, (n_sb, tq), 0)
    tpos = qi * tq + lax.broadcasted_iota(jnp.int32, (n_sb, tq), 1)
    cur = tpos // SEL_BLOCK
    forced = (blk == 0) | (blk == cur) | (blk == cur - 1)
    future = blk * SEL_BLOCK > tpos

    o_c, sel_bf = [], []
    for g in groups:
        s = _dot_nt(qs[g], kc_ref[0]).reshape(HPG_A, tq, rows_c) + bc_ref[g * HPG_A:(g + 1) * HPG_A, 0]
        m = jnp.max(s, axis=-1, keepdims=True)
        e = jnp.where(s > 0.5 * NEG, jnp.exp2(s - m), 0.0)
        e_bf = e.reshape(rows, rows_c).astype(BF16)
        inv = 1.0 / jnp.maximum(row_sums(e_bf), 1e-30)
        o_c.append(_dot(e_bf, vc_ref[0]) * inv)
        inv4 = inv.reshape(HPG_A, tq, LANES)
        p = e * jnp.concatenate([inv4] * (rows_c // LANES), axis=2)
        p_sum = p[0] + p[1] + p[2] + p[3]
        p_hi, p_lo = _split_bf16(p_sum)
        p_lo2 = (p_sum - p_hi.astype(F32) - p_lo.astype(F32)).astype(BF16)
        imp = _dot_nt(ov_t, p_hi) + _dot_nt(ov_t, p_lo) + _dot_nt(ov_t, p_lo2)
        imp = jnp.where(forced, 1e6, jnp.where(future, -1e6, imp))
        imp_ref[...] = imp

        def count_beaten(i0, cnt):
            for u in range(tq // SEL_BLOCK):
                i = i0 * (tq // SEL_BLOCK) + u
                ri = jnp.broadcast_to(imp_ref[pl.ds(i, 1), :], (n_sb, tq))
                later = jnp.where(blk > i, 1.0, 0.0)
                cnt = cnt + jnp.where(ri > imp, 1.0, jnp.where(ri == imp, later, 0.0))
            return cnt

        cnt = lax.fori_loop(0, qi + 1, count_beaten, jnp.zeros((n_sb, tq), F32))
        sel_bf.append(jnp.where(cnt < k_sel, 1.0, 0.0).T.astype(BF16))

    def with_ones(v_pair, g):
        return v_pair * half_bf[g] + half_bf[1 - g]

    def normalised(acc):
        return acc / pltpu.roll(acc, HEAD_DIM, axis=1)

    def sel_body(kj, carry):
        r0 = pl.multiple_of(kj * tk, tk)
        k = ksw_ref[pl.ds(r0, tk), 0:KV_A]
        v = ksw_ref[pl.ds(r0, tk), 2 * KV_A:3 * KV_A]
        out = []
        for g in groups:
            m, acc = carry[g]
            s = _dot_nt(qs[g], k).reshape(HPG_A, tq, tk)
            s = s + _toeplitz_bias(bs_ref, g, (tq // TOEP) * qi - (tk // TOEP) * kj, tq // TOEP, tk // TOEP)
            madd = (_dot(sel_bf[g], ex_ref[kj]) - 1.0) * (-NEG)
            s = (s + madd[None]).reshape(rows, tk)
            m_new = jnp.maximum(m, jnp.max(s, axis=-1, keepdims=True))
            p = jnp.exp2(s - m_new).astype(BF16)
            out.append((m_new, jnp.exp2(m - m_new) * acc + _dot(p, with_ones(v, g))))
        return tuple(out)

    init = tuple((jnp.full((rows, 1), NEG, F32), jnp.zeros((rows, LANES), F32)) for _ in groups)
    n_tiles = (qi * tq + tq + tk - 1) // tk
    sel_out = lax.fori_loop(0, n_tiles, sel_body, init)
    o_s = [normalised(acc) for _, acc in sel_out]

    n_wk = WIN_A + tq
    start = pl.multiple_of(jnp.maximum(qi * tq - WIN_A, 0), tq)
    kw = ksw_ref[pl.ds(start, n_wk), KV_A:2 * KV_A]
    vw = ksw_ref[pl.ds(start, n_wk), 3 * KV_A:4 * KV_A]
    o_w = []
    for g in groups:
        s = _dot_nt(qs[g], kw).reshape(HPG_A, tq, n_wk)
        s = s + _toeplitz_bias(bw_ref, g, (qi * tq - start) // TOEP, tq // TOEP, n_wk // TOEP)
        s = s.reshape(rows, n_wk)
        p = jnp.exp2(s - jnp.max(s, axis=-1, keepdims=True)).astype(BF16)
        o_w.append(normalised(_dot(p, with_ones(vw, g))))

    g_hi, g_lo = _split_bf16(ga_ref[...])
    gates = _dot(g_hi, gp_ref[...]) + _dot(g_lo, gp_ref[...])
    for c in range(HPG_A):
        rs = slice(c * tq, (c + 1) * tq)
        out = jnp.zeros((tq, LANES), F32)
        for j, o in enumerate((o_c, o_s, o_w)):
            gate = gates[:, j * WIDTH_A + c * LANES:j * WIDTH_A + (c + 1) * LANES]
            out = out + gate * jnp.where(low_half, o[0][rs], o[1][rs])
        o_ref[:, c * LANES:(c + 1) * LANES] = out


def _nsa_attention(qa, kc, vc, ksw, ga, bias_c, bias_s, bias_w, batch, seq):
    tq, tk = TQ_NSA, TK_NSA
    nq = seq // tq
    n = batch * seq
    rows_c = seq // CMP_STRIDE
    n_sb = seq // SEL_BLOCK
    assert rows_c % LANES == 0 and seq % tk == 0 and seq >= WIN_A + tq and WIN_A % tq == 0
    key_blk = (np.arange(seq) // SEL_BLOCK).reshape(seq // tk, 1, tk)
    expand = jnp.asarray((key_blk == np.arange(n_sb).reshape(1, n_sb, 1)).astype(np.float32), BF16)
    gp = np.zeros((LANES, 3 * WIDTH_A), np.float32)
    for pos, head in enumerate(PAIRED_HEADS_A):
        for j in range(3):
            gp[3 * head + j, j * WIDTH_A + pos * HEAD_DIM:j * WIDTH_A + (pos + 1) * HEAD_DIM] = 1.0
    gp = jnp.asarray(gp, BF16)
    resident = lambda a: pl.BlockSpec(a.shape, lambda b, i: (0,) * a.ndim, pipeline_mode=pl.Buffered(1))
    return pl.pallas_call(
        functools.partial(_nsa_kernel, seq=seq),
        grid=(batch, nq),
        in_specs=[
            pl.BlockSpec((tq, WIDTH_A), lambda b, i: (b * nq + i, 0)),
            pl.BlockSpec((1, rows_c, KV_A), lambda b, i: (b, 0, 0)),
            pl.BlockSpec((1, rows_c, KV_A), lambda b, i: (b, 0, 0)),
            pl.BlockSpec((seq, 4 * KV_A), lambda b, i: (b, 0)),
            pl.BlockSpec((tq, LANES), lambda b, i: (b * nq + i, 0)),
            pl.BlockSpec((N_HEADS_A, 1, tq, rows_c), lambda b, i: (0, i, 0, 0)),
            resident(bias_s), resident(bias_w), resident(expand), resident(gp),
        ],
        out_specs=pl.BlockSpec((tq, WIDTH_A), lambda b, i: (b * nq + i, 0)),
        out_shape=jax.ShapeDtypeStruct((n, WIDTH_A), F32),
        scratch_shapes=[pltpu.VMEM((n_sb, tq), F32)],
        compiler_params=_cparams(("parallel", "arbitrary")),
        name="nsa_attention",
    )(qa, kc, vc, ksw, ga, bias_c, bias_s, bias_w, expand, gp)


def _dilated_kernel(q_ref, kp_ref, kc_ref, vp_ref, vc_ref, bias_ref, o_ref, lse_ref):
    band = DIL_BAND
    first = pl.program_id(2) == 0
    prev_mask = jnp.where(first, NEG, 0.0)
    lane = lax.broadcasted_iota(jnp.int32, (1, LANES), 1)
    low_half = lane < HEAD_DIM
    half_bf = [jnp.where(low_half, 1.0, 0.0).astype(BF16), jnp.where(low_half, 0.0, 1.0).astype(BF16)]
    den_sel = [jnp.broadcast_to(hm, (2 * band, LANES)) for hm in half_bf]
    key_col = lax.broadcasted_iota(jnp.int32, (1, 2 * band), 1)
    prev_cols = jnp.where(key_col < band, prev_mask, 0.0)
    n_cls, n_sub = q_ref.shape[0], q_ref.shape[1] // band
    for cl, sub in ((cl, sub) for cl in range(n_cls) for sub in range(n_sub)):
        r_cur = slice(sub * band, (sub + 1) * band)
        for c in range(WIDTH_B // LANES):
            cols = slice(c * LANES, (c + 1) * LANES)
            q = q_ref[cl, r_cur, cols]
            if sub == 0:
                k_cat = jnp.concatenate([kp_ref[cl, :, cols], kc_ref[cl, r_cur, cols]], axis=0)
                v_cat = jnp.concatenate([vp_ref[cl, :, cols], vc_ref[cl, r_cur, cols]], axis=0)
            else:
                k_cat = kc_ref[cl, (sub - 1) * band:(sub + 1) * band, cols]
                v_cat = vc_ref[cl, (sub - 1) * band:(sub + 1) * band, cols]
            acc = jnp.zeros((band, 2 * LANES), F32)
            ms = []
            for j in range(2):
                h = 2 * c + j
                s = _dot_nt(q * half_bf[j], k_cat) + bias_ref[h, 0]
                if sub == 0:
                    s = s + prev_cols
                m = jnp.max(s, axis=-1, keepdims=True)
                e = jnp.exp2(s - m).astype(BF16)
                acc = acc + _dot(e, jnp.concatenate([v_cat * half_bf[j], den_sel[j]], axis=1))
                ms.append(m)
            den = acc[:, LANES:]
            o_ref[cl, r_cur, cols] = acc[:, :LANES] / den
            lse_ref[cl, r_cur, cols] = jnp.where(low_half, ms[0], ms[1]) + jnp.log2(den)


def _dilated_attention(q, k, v, bias, dil):
    band = DIL_BAND
    batch, _, length, _ = q.shape
    rows = min(DIL_ROWS, length)
    n_cls = min(dil, DIL_ROWS // rows)
    sub = rows // band
    cur = pl.BlockSpec((None, n_cls, rows, WIDTH_B), lambda b, r, n: (b, r, n, 0))
    prev = pl.BlockSpec((None, n_cls, band, WIDTH_B), lambda b, r, n: (b, r, jnp.maximum(n * sub - 1, 0), 0))
    return pl.pallas_call(
        _dilated_kernel,
        grid=(batch, dil // n_cls, length // rows),
        in_specs=[cur, prev, cur, prev, cur,
                  pl.BlockSpec(bias.shape, lambda b, r, n: (0, 0, 0, 0))],
        out_specs=[cur, cur],
        out_shape=[jax.ShapeDtypeStruct((batch, dil, length, WIDTH_B), F32)] * 2,
        compiler_params=_cparams(("parallel", "parallel", "arbitrary")),
        name=f"dilated_attention_d{dil}",
    )(q, k, k, v, v, bias)


def _outproj_kernel(*refs, with_router):
    if with_router:
        (oa_ref, o1_ref, o2_ref, o3_ref, l1_ref, l2_ref, l3_ref, h_ref, gn_ref, w_ref, fg_ref, rw_ref,
         h_out_ref, v_ref, route_ref, stage_ref) = refs
    else:
        (oa_ref, o1_ref, o2_ref, o3_ref, l1_ref, l2_ref, l3_ref, h_ref, gn_ref, w_ref, fg_ref,
         h_out_ref, v_ref, stage_ref) = refs
    tm = h_ref.shape[0]

    def token_order(ref, slot):
        r = ref.shape[1]
        if r == 1:
            return ref[0, 0]
        n_chunks = ref.shape[3] // LANES
        for rho in range(r):
            for c in range(n_chunks):
                stage_ref[slot, c, pl.ds(rho, tm // r, stride=r), :] = ref[0, rho, :, c * LANES:(c + 1) * LANES]
        return jnp.concatenate([stage_ref[slot, c] for c in range(n_chunks)], axis=1)

    l1, l2, l3 = token_order(l1_ref, 0), token_order(l2_ref, 0), token_order(l3_ref, 1)
    m = jnp.maximum(jnp.maximum(l1, l2), l3)
    e1, e2, e3 = jnp.exp2(l1 - m), jnp.exp2(l2 - m), jnp.exp2(l3 - m)
    o1, o2, o3 = token_order(o1_ref, 0), token_order(o2_ref, 2), token_order(o3_ref, 3)
    ob = (e1 * o1 + e2 * o2 + e3 * o3) / (e1 + e2 + e3)

    def norm(x, g):
        return x * lax.rsqrt(jnp.mean(x * x, axis=-1, keepdims=True) + RMS_EPS) * g

    o = jnp.concatenate([norm(oa_ref[...], gn_ref[:, :WIDTH_A]), norm(ob, gn_ref[:, WIDTH_A:])], axis=1)
    h = h_ref[...] + _dot(o.astype(BF16), w_ref[...])
    h_out_ref[...] = h
    v = norm(h, fg_ref[...])
    v_ref[...] = v.astype(v_ref.dtype)
    if with_router:
        v_hi, v_lo = _split_bf16(v)
        w_hi, w_lo = _split_bf16(rw_ref[...])
        logits = _dot(v_hi, w_hi) + _dot(v_lo, w_hi) + _dot(v_hi, w_lo)
        lg = logits.T[:N_EXPERTS]
        row = lax.broadcasted_iota(jnp.int32, lg.shape, 0).astype(F32)
        m1 = jnp.max(lg, axis=0, keepdims=True)
        i1 = jnp.min(jnp.where(lg == m1, row, float(N_EXPERTS)), axis=0, keepdims=True)
        lg2 = jnp.where(row == i1, -jnp.inf, lg)
        m2 = jnp.max(lg2, axis=0, keepdims=True)
        i2 = jnp.min(jnp.where(lg2 == m2, row, float(N_EXPERTS)), axis=0, keepdims=True)
        e = jnp.exp(m2 - m1)
        g1 = 1.0 / (1.0 + e)
        g2 = e / (1.0 + e)
        route_ref[...] = jnp.where(row == 0, i1, jnp.where(
            row == 1, i2, jnp.where(row == 2, g1, jnp.where(row == 3, g2, 0.0))))


def _out_projection(o_a, dil_outs, h, out_norm, w_out, ffn_norm, router_w):
    n = h.shape[0]
    tm = TM_PROJ
    with_router = router_w is not None
    row = lambda width: pl.BlockSpec((tm, width), lambda i: (i, 0))
    full = lambda a: pl.BlockSpec(a.shape, lambda i: (0,) * a.ndim)
    (o1, l1), (o2, l2), (o3, l3) = dil_outs
    gn_a = out_norm[:WIDTH_A].reshape(N_HEADS_A, HEAD_DIM)[PAIRED_HEADS_A].reshape(WIDTH_A)
    gn = jnp.concatenate([gn_a, out_norm[WIDTH_A:]]).reshape(1, -1)
    fg = ffn_norm.reshape(1, -1)
    w_a = w_out[:WIDTH_A].reshape(N_HEADS_A, HEAD_DIM, D_MODEL)[PAIRED_HEADS_A].reshape(WIDTH_A, D_MODEL)
    w = jnp.concatenate([w_a, w_out[WIDTH_A:]], axis=0).astype(BF16)
    tiles_per_seq = o1.shape[2] // tm
    grouped = lambda a: pl.BlockSpec((1, a.shape[1], tm // a.shape[1], WIDTH_B),
                                     lambda i: (i // tiles_per_seq, 0, i % tiles_per_seq, 0))
    args = [o_a, o1, o2, o3, l1, l2, l3, h, gn, w, fg]
    in_specs = ([row(WIDTH_A)] + [grouped(a) for a in args[1:7]]
                + [row(D_MODEL), full(gn), full(w), full(fg)])
    out_specs = [row(D_MODEL), row(D_MODEL)]
    out_shape = [jax.ShapeDtypeStruct((n, D_MODEL), F32),
                 jax.ShapeDtypeStruct((n, D_MODEL), F32 if with_router else BF16)]
    if with_router:
        rw = jnp.pad(router_w, ((0, 0), (0, LANES - N_EXPERTS)))
        args.append(rw)
        in_specs.append(full(rw))
        out_specs.append(pl.BlockSpec((N_EXPERTS, tm), lambda i: (0, i)))
        out_shape.append(jax.ShapeDtypeStruct((N_EXPERTS, n), F32))
    return pl.pallas_call(
        functools.partial(_outproj_kernel, with_router=with_router),
        grid=(n // tm,),
        in_specs=in_specs,
        out_specs=out_specs,
        out_shape=out_shape,
        scratch_shapes=[pltpu.VMEM((4, WIDTH_B // LANES, tm, LANES), F32)],
        compiler_params=_cparams(("parallel",)),
        name="out_projection",
    )(*args)


def _ffn_kernel(v_ref, h_ref, wg_ref, wu_ref, wd_ref, o_ref, acc_ref):
    f = pl.program_id(1)

    @pl.when(f == 0)
    def _():
        acc_ref[...] = jnp.zeros_like(acc_ref)

    v = v_ref[...]
    gate = _dot(v, wg_ref[...])
    up = _dot(v, wu_ref[...])
    acc_ref[...] += _dot((_silu(gate) * up).astype(BF16), wd_ref[...])

    @pl.when(f == pl.num_programs(1) - 1)
    def _():
        o_ref[...] = h_ref[...] + acc_ref[...]


def _dense_ffn(v, h, w_gate, w_up, w_down):
    n = h.shape[0]
    d_ff = w_gate.shape[1]
    tm, tf = TM_FFN, TF_FFN
    return pl.pallas_call(
        _ffn_kernel,
        grid=(n // tm, d_ff // tf),
        in_specs=[pl.BlockSpec((tm, D_MODEL), lambda i, f: (i, 0)),
                  pl.BlockSpec((tm, D_MODEL), lambda i, f: (i, 0)),
                  pl.BlockSpec((D_MODEL, tf), lambda i, f: (0, f)),
                  pl.BlockSpec((D_MODEL, tf), lambda i, f: (0, f)),
                  pl.BlockSpec((tf, D_MODEL), lambda i, f: (f, 0))],
        out_specs=pl.BlockSpec((tm, D_MODEL), lambda i, f: (i, 0)),
        out_shape=jax.ShapeDtypeStruct((n, D_MODEL), F32),
        scratch_shapes=[pltpu.VMEM((tm, D_MODEL), F32)],
        compiler_params=_cparams(("parallel", "arbitrary")),
        name="dense_ffn",
    )(v, h, w_gate.astype(BF16), w_up.astype(BF16), w_down.astype(BF16))


def _row_copy(src_ref, src_row, dst_ref, dst_row, sem):
    return pltpu.make_async_copy(src_ref.at[pl.ds(src_row, 1)], dst_ref.at[pl.ds(dst_row, 1)], sem)


def _dispatch_kernel(dest_ref, v_ref, xs_in_ref, xs_ref, sem):
    del xs_in_ref
    n_slots = dest_ref.shape[2]

    def issue(i, carry):
        for k in range(TOP_K):
            _row_copy(v_ref, i, xs_ref, dest_ref[0, 0, TOP_K * i + k], sem).start()
        return carry

    lax.fori_loop(0, n_slots // TOP_K, issue, 0, unroll=DMA_ISSUE_UNROLL)
    pltpu.make_async_copy(xs_ref.at[pl.ds(0, n_slots)], xs_ref.at[pl.ds(0, n_slots)], sem).wait()


def _dispatch(v, dest, n_rows):
    n = v.shape[0]
    tokens = DISPATCH_TOKENS
    slots = tokens * TOP_K
    return pl.pallas_call(
        _dispatch_kernel,
        grid=(n // tokens,),
        in_specs=[pl.BlockSpec((1, 1, slots), lambda i: (i, 0, 0), memory_space=pltpu.SMEM),
                  pl.BlockSpec((tokens, D_MODEL), lambda i: (i, 0)),
                  pl.BlockSpec(memory_space=pl.ANY)],
        out_specs=pl.BlockSpec(memory_space=pl.ANY),
        out_shape=jax.ShapeDtypeStruct((n_rows, D_MODEL), F32),
        scratch_shapes=[pltpu.SemaphoreType.DMA(())],
        input_output_aliases={2: 0},
        compiler_params=_cparams(("arbitrary",)),
        name="moe_dispatch",
    )(dest.reshape(n // tokens, 1, slots), v, jnp.zeros((n_rows, D_MODEL), F32))


def _gmm_kernel(te_ref, tv_ref, x_ref, wg_ref, wu_ref, wd_ref, y_ref, xb_ref, acc_ref):
    c = pl.program_id(0)
    f = pl.program_id(1)
    valid = tv_ref[c] > 0

    @pl.when(f == 0)
    def _():
        acc_ref[...] = jnp.zeros_like(acc_ref)
        xb_ref[...] = x_ref[...].astype(BF16)

    @pl.when(valid)
    def _():
        x = xb_ref[...]
        gate = _dot(x, wg_ref[0])
        up = _dot(x, wu_ref[0])
        acc_ref[...] += _dot((_silu(gate) * up).astype(BF16), wd_ref[0])

    @pl.when(f == pl.num_programs(1) - 1)
    def _():
        y_ref[...] = acc_ref[...]


def _grouped_swiglu(xs, tile_expert, tile_valid, w_gate, w_up, w_down):
    n_rows = xs.shape[0]
    d_ff = w_gate.shape[2]
    tm, tf = TM_MOE, TF_MOE
    grid_spec = pltpu.PrefetchScalarGridSpec(
        num_scalar_prefetch=2,
        grid=(n_rows // tm, d_ff // tf),
        in_specs=[pl.BlockSpec((tm, D_MODEL), lambda c, f, te, tv: (c, 0)),
                  pl.BlockSpec((1, D_MODEL, tf), lambda c, f, te, tv: (te[c], 0, f)),
                  pl.BlockSpec((1, D_MODEL, tf), lambda c, f, te, tv: (te[c], 0, f)),
                  pl.BlockSpec((1, tf, D_MODEL), lambda c, f, te, tv: (te[c], f, 0))],
        out_specs=pl.BlockSpec((tm, D_MODEL), lambda c, f, te, tv: (c, 0)),
        scratch_shapes=[pltpu.VMEM((tm, D_MODEL), BF16), pltpu.VMEM((tm, D_MODEL), F32)],
    )
    return pl.pallas_call(
        _gmm_kernel,
        grid_spec=grid_spec,
        out_shape=jax.ShapeDtypeStruct((n_rows, D_MODEL), F32),
        compiler_params=_cparams(("parallel", "arbitrary")),
        name="moe_grouped_swiglu",
    )(tile_expert, tile_valid, xs, w_gate.astype(BF16), w_up.astype(BF16), w_down.astype(BF16))


def _combine_kernel(dest_ref, dest_next_ref, y_ref, gate_ref, h_ref, o_ref, buf_ref, sem):
    tc = h_ref.shape[0]
    step = pl.program_id(0)
    slot = step % 2

    def start_gather(idx_ref, into):
        def issue(i, carry):
            for k in range(TOP_K):
                _row_copy(y_ref, idx_ref[0, 0, TOP_K * i + k], buf_ref.at[into, k], i, sem.at[into]).start()
            return carry

        lax.fori_loop(0, tc, issue, 0, unroll=DMA_ISSUE_UNROLL)

    @pl.when(step == 0)
    def _():
        start_gather(dest_ref, 0)

    @pl.when(step + 1 < pl.num_programs(0))
    def _():
        start_gather(dest_next_ref, 1 - slot)

    for k in range(TOP_K):
        pltpu.make_async_copy(y_ref.at[pl.ds(0, tc)], buf_ref.at[slot, k], sem.at[slot]).wait()
    g = gate_ref[...]
    o_ref[...] = h_ref[...] + g[:, 0:1] * buf_ref[slot, 0] + g[:, 1:2] * buf_ref[slot, 1]


def _combine(y, dest, gates, h):
    n = h.shape[0]
    tc = COMBINE_TOKENS
    steps = n // tc
    dest = dest.reshape(steps, 1, tc * TOP_K)
    return pl.pallas_call(
        _combine_kernel,
        grid=(steps,),
        in_specs=[pl.BlockSpec((1, 1, tc * TOP_K), lambda i: (i, 0, 0), memory_space=pltpu.SMEM),
                  pl.BlockSpec((1, 1, tc * TOP_K), lambda i: (jnp.minimum(i + 1, steps - 1), 0, 0),
                               memory_space=pltpu.SMEM),
                  pl.BlockSpec(memory_space=pl.ANY),
                  pl.BlockSpec((tc, TOP_K), lambda i: (i, 0)),
                  pl.BlockSpec((tc, D_MODEL), lambda i: (i, 0))],
        out_specs=pl.BlockSpec((tc, D_MODEL), lambda i: (i, 0)),
        out_shape=jax.ShapeDtypeStruct((n, D_MODEL), F32),
        scratch_shapes=[pltpu.VMEM((2, TOP_K, tc, D_MODEL), F32), pltpu.SemaphoreType.DMA((2,))],
        compiler_params=_cparams(("arbitrary",)),
        name="moe_combine",
    )(dest, dest, y, gates, h)


def _moe(v, route, h, w_gate, w_up, w_down):
    n = h.shape[0]
    nk = n * TOP_K
    tm = TM_MOE
    e_flat = route[:TOP_K].T.astype(jnp.int32).reshape(nk)
    gates = route[TOP_K:2 * TOP_K].T
    onehot = (e_flat[:, None] == jnp.arange(N_EXPERTS)[None, :]).astype(jnp.int32)
    csum = jnp.cumsum(onehot, axis=0)
    counts = csum[-1]
    rank = jnp.sum(jnp.where(onehot > 0, csum, 0), axis=1) - 1
    padded = (counts + tm - 1) // tm * tm
    pend = jnp.cumsum(padded)
    pstart = pend - padded
    dest = (jnp.sum(jnp.where(onehot > 0, pstart[None, :], 0), axis=1) + rank).astype(jnp.int32)
    n_tiles = nk // tm + N_EXPERTS
    tile_start = jnp.arange(n_tiles) * tm
    tile_expert = jnp.minimum(jnp.sum(tile_start[:, None] >= pend[None, :], axis=1), N_EXPERTS - 1)
    tile_valid = (tile_start < pend[-1]).astype(jnp.int32)
    xs = _dispatch(v, dest, n_tiles * tm)
    y = _grouped_swiglu(xs, tile_expert.astype(jnp.int32), tile_valid, w_gate, w_up, w_down)
    return _combine(y, dest, gates, h)


def kernel(x, rel_bias, attn_norm, w_in, nsa_q_norm, nsa_k_norm, cmp_pos, cmp_w1, cmp_b1, cmp_w2,
           dil_q_norm, dil_k_norm, out_norm, w_out, ffn_norm, ffn_w_gate, ffn_w_up, ffn_w_down,
           router_w, exp_w_gate, exp_w_up, exp_w_down):
    batch, seq, _ = x.shape
    depth = attn_norm.shape[0]
    n = batch * seq
    assert all((seq // d) % min(DIL_ROWS, seq // d) == 0 and (seq // d) % DIL_BAND == 0 for _, d in DIL_PAIRS)
    assert seq % TM_PROJ == 0 and n % DISPATCH_TOKENS == 0
    assert all(w == d * DIL_BAND for w, d in DIL_PAIRS)

    tbl = rel_bias.astype(F32).T
    tbl = tbl * LOG2E
    tbl_a, tbl_b = tbl[:N_HEADS_A], tbl[N_HEADS_A:]
    max_off = seq // TOEP - 1
    far_sel = min(max_off, -(-(FAR_DIST + TOEP - 1) // TOEP))
    far_win = min(max_off, -(-(WIN_A + TOEP - 1) // TOEP))
    bias_c = _bias_table(tbl_a, seq // TQ_NSA, TQ_NSA, seq // CMP_STRIDE,
                         functools.partial(_cmp_dist, n_cmp=(seq - CMP_LEN) // CMP_STRIDE + 1))
    bias_s = _bias_table(tbl_a, far_sel + 2, TOEP, TOEP, _sel_dist)
    bias_w = _bias_table(tbl_a, far_win + 2, TOEP, TOEP, _win_dist)
    bias_d = [_bias_table(tbl_b, 1, DIL_BAND, 2 * DIL_BAND, functools.partial(_dil_dist, dil=d))
              for _, d in DIL_PAIRS]

    h = x.reshape(n, D_MODEL)
    for layer in range(depth):
        qa, cva, ksw, ga, qkv_dilated = _in_projection(
            h, seq, attn_norm[layer], w_in[layer], nsa_q_norm[layer], nsa_k_norm[layer],
            dil_q_norm[layer], dil_k_norm[layer])
        kc, vc = _compress(cva, cmp_pos[layer], cmp_w1[layer], cmp_b1[layer], cmp_w2[layer],
                           nsa_k_norm[layer, 0])
        o_a = _nsa_attention(qa, kc, vc, ksw, ga, bias_c, bias_s, bias_w, batch, seq)
        dil_outs = [_dilated_attention(*qkv_dilated[i], bias_d[i], d) for i, (_, d) in enumerate(DIL_PAIRS)]
        moe_layer = layer % 2 == 1
        outs = _out_projection(o_a, dil_outs, h, out_norm[layer], w_out[layer], ffn_norm[layer],
                               router_w[layer // 2] if moe_layer else None)
        if moe_layer:
            h1, v, route = outs
            h = _moe(v, route, h1, exp_w_gate[layer // 2], exp_w_up[layer // 2], exp_w_down[layer // 2])
        else:
            h1, v = outs
            h = _dense_ffn(v, h1, ffn_w_gate[layer // 2], ffn_w_up[layer // 2], ffn_w_down[layer // 2])
    return h.reshape(batch, seq, D_MODEL)
```
